```python
import jax, jax.numpy as jnp
from jax import lax
import numpy as np

D_MODEL = 1024
BATCH = 4
SEQ = 4096
DEPTH = 1
DEC_BATCH = 32
DEC_SEQ = 64
PAST_LEN = 1024

CHUNK = 64
Q_BLOCK = 128
HEAD_DIM = 64
N_HEADS_FOX = 8
N_HEADS_DSA = 8
D_FOX = N_HEADS_FOX * HEAD_DIM
D_DSA = N_HEADS_DSA * HEAD_DIM
D_MIX = D_FOX + D_DSA
N_IDX_HEADS = 8
IDX_DIM = 32
TOPK_MAX = 256
D_FF = 4 * D_MODEL
PLE_DIM = 256
ROPE_THETA = 10000.0
LN_EPS = 1e-5
ALPHA = (2 * DEPTH) ** 0.25
BETA = (8 * DEPTH) ** -0.25
NEG = -1e30

SPLIT_POINTS = (
    D_FOX,
    2 * D_FOX,
    3 * D_FOX,
    3 * D_FOX + N_HEADS_FOX,
    3 * D_FOX + N_HEADS_FOX + D_DSA,
    3 * D_FOX + N_HEADS_FOX + 2 * D_DSA,
    3 * D_FOX + N_HEADS_FOX + 3 * D_DSA,
    3 * D_FOX + N_HEADS_FOX + 3 * D_DSA + N_IDX_HEADS * IDX_DIM,
    3 * D_FOX + N_HEADS_FOX + 3 * D_DSA + N_IDX_HEADS * IDX_DIM + IDX_DIM,
)
N_IN = 3 * D_FOX + N_HEADS_FOX + 3 * D_DSA + N_IDX_HEADS * IDX_DIM + IDX_DIM + N_IDX_HEADS

kernel_name = 'hybrid_fox_dsa_streaming_step'


def layer_norm(x, g, b):
    xf = x.astype(jnp.float32)
    mu = jnp.mean(xf, axis=-1, keepdims=True)
    var = jnp.mean(jnp.square(xf - mu), axis=-1, keepdims=True)
    return ((xf - mu) * lax.rsqrt(var + LN_EPS) * g + b).astype(x.dtype)


def rope(x, pos):
    half = x.shape[-1] // 2
    inv_freq = ROPE_THETA ** (-jnp.arange(half, dtype=jnp.float32) / half)
    ang = pos.astype(jnp.float32)[:, None] * inv_freq[None, :]
    cos = jnp.cos(ang)[None, :, None, :]
    sin = jnp.sin(ang)[None, :, None, :]
    xf = x.astype(jnp.float32)
    x1, x2 = xf[..., :half], xf[..., half:]
    return jnp.concatenate([x1 * cos - x2 * sin, x2 * cos + x1 * sin], axis=-1).astype(x.dtype)


def project_mixers(x, w_in, b_f, pos):
    B, L, _ = x.shape
    h = jnp.einsum('bld,dn->bln', x, w_in)
    fq, fk, fv, fg, dq, dk, dv, iq, ik, iw = jnp.split(h, SPLIT_POINTS, axis=-1)
    fq = fq.reshape(B, L, N_HEADS_FOX, HEAD_DIM)
    fk = fk.reshape(B, L, N_HEADS_FOX, HEAD_DIM)
    fv = fv.reshape(B, L, N_HEADS_FOX, HEAD_DIM)
    logf = jax.nn.log_sigmoid((fg + b_f).astype(jnp.float32))
    dq = rope(dq.reshape(B, L, N_HEADS_DSA, HEAD_DIM), pos)
    dk = rope(dk.reshape(B, L, N_HEADS_DSA, HEAD_DIM), pos)
    dv = dv.reshape(B, L, N_HEADS_DSA, HEAD_DIM)
    iq = rope(iq.reshape(B, L, N_IDX_HEADS, IDX_DIM), pos)
    ik = rope(ik[:, :, None, :], pos)[:, :, 0, :]
    iw = iw * (N_IDX_HEADS ** -0.5 * IDX_DIM ** -0.5)
    return fq, fk, fv, logf, dq, dk, dv, iq, ik, iw


def fox_attend(q, k, v, cum_q, cum_k, q_pos, k_pos):
    s = jnp.einsum('bqhd,blhd->bhql', q, k, preferred_element_type=jnp.float32) * HEAD_DIM ** -0.5
    decay = jnp.transpose(cum_q, (0, 2, 1))[:, :, :, None] - jnp.transpose(cum_k, (0, 2, 1))[:, :, None, :]
    mask = k_pos[None, :] <= q_pos[:, None]
    p = jax.nn.softmax(jnp.where(mask, s + decay, NEG), axis=-1)
    return jnp.einsum('bhql,blhd->bqhd', p.astype(v.dtype), v)


def _gather_rows(a, idx):
    return a[idx]


def dsa_attend(q, k, v, iq, ik, iw, q_pos, k_pos, topk):
    logits = jnp.einsum('bqhd,bld->bqhl', iq, ik, preferred_element_type=jnp.float32)
    score = jnp.einsum('bqh,bqhl->bql', iw.astype(jnp.float32), jax.nn.relu(logits))
    admissible = (k_pos[None, :] // CHUNK) <= (q_pos[:, None] // CHUNK)
    score = jnp.where(admissible[None], score, NEG)
    _, sel = lax.top_k(score, topk)
    valid = (k_pos[sel] // CHUNK) <= (q_pos[None, :, None] // CHUNK)
    kg = jax.vmap(_gather_rows)(k, sel)
    vg = jax.vmap(_gather_rows)(v, sel)
    s = jnp.einsum('bqhd,bqkhd->bhqk', q, kg, preferred_element_type=jnp.float32) * HEAD_DIM ** -0.5
    p = jax.nn.softmax(jnp.where(valid[:, None], s, NEG), axis=-1)
    return jnp.einsum('bhqk,bqkhd->bqhd', p.astype(vg.dtype), vg)


def prompt_mixer(x, w_in, b_f):
    B, L, _ = x.shape
    pos = jnp.arange(L, dtype=jnp.int32)
    fq, fk, fv, logf, dq, dk, dv, iq, ik, iw = project_mixers(x, w_in, b_f, pos)
    cum = jnp.cumsum(logf, axis=1)
    topk = min(TOPK_MAX, L // 4)
    nb = L // Q_BLOCK

    def to_blocks(a):
        return jnp.swapaxes(a.reshape((B, nb, Q_BLOCK) + a.shape[2:]), 0, 1)

    def block_fn(args):
        fq_b, cum_b, dq_b, iq_b, iw_b, pos_b = args
        o_fox = fox_attend(fq_b, fk, fv, cum_b, cum, pos_b, pos)
        o_dsa = dsa_attend(dq_b, dk, dv, iq_b, ik, iw_b, pos_b, pos, topk)
        return jnp.concatenate([o_fox.reshape(B, Q_BLOCK, D_FOX), o_dsa.reshape(B, Q_BLOCK, D_DSA)], axis=-1)

    out = lax.map(block_fn, (to_blocks(fq), to_blocks(cum), to_blocks(dq), to_blocks(iq),
                             to_blocks(iw), pos.reshape(nb, Q_BLOCK)))
    mix = jnp.swapaxes(out, 0, 1).reshape(B, L, D_MIX)
    return mix, (fk, fv, logf, dk, dv, ik)


def sample_mixer(x, w_in, b_f, c_fk, c_fv, c_logf, c_dk, c_dv, c_ik):
    B, T, _ = x.shape
    P = c_fk.shape[1]
    pos_new = P + jnp.arange(T, dtype=jnp.int32)
    k_pos = jnp.arange(P + T, dtype=jnp.int32)
    fq, fk, fv, logf, dq, dk, dv, iq, ik, iw = project_mixers(x, w_in, b_f, pos_new)
    kf = jnp.concatenate([c_fk.astype(fk.dtype), fk], axis=1)
    vf = jnp.concatenate([c_fv.astype(fv.dtype), fv], axis=1)
    cum = jnp.cumsum(jnp.concatenate([c_logf.astype(jnp.float32), logf], axis=1), axis=1)
    kd = jnp.concatenate([c_dk.astype(dk.dtype), dk], axis=1)
    vd = jnp.concatenate([c_dv.astype(dv.dtype), dv], axis=1)
    ki = jnp.concatenate([c_ik.astype(ik.dtype), ik], axis=1)
    topk = min(TOPK_MAX, (P + T) // 4)
    o_fox = fox_attend(fq, kf, vf, cum[:, P:], cum, pos_new, k_pos)
    o_dsa = dsa_attend(dq, kd, vd, iq, ki, iw, pos_new, k_pos, topk)
    mix = jnp.concatenate([o_fox.reshape(B, T, D_FOX), o_dsa.reshape(B, T, D_DSA)], axis=-1)
    return mix, (fk, fv, logf, dk, dv, ik)


def finish_layer(x, mix, pe, w_o, ln1_g, ln1_b, w_up, w_down, ln2_g, ln2_b, w_ple, w_ple_gate, b_ple_gate):
    x = layer_norm(ALPHA * x + jnp.einsum('blm,md->bld', mix, w_o), ln1_g, ln1_b)
    hid = jnp.square(jax.nn.relu(jnp.einsum('bld,df->blf', x, w_up)))
    x = layer_norm(ALPHA * x + jnp.einsum('blf,fd->bld', hid, w_down), ln2_g, ln2_b)
    gate = jax.nn.sigmoid(jnp.einsum('bld,de->ble', x, w_ple_gate) + b_ple_gate)
    return x + gate * jnp.einsum('blp,pd->bld', pe, w_ple)


def setup_inputs(seed: int = 0) -> dict:
    key = jax.random.key(seed)
    ks = jax.random.split(key, 24)
    f32 = jnp.float32
    nrm = lambda k, shape: jax.random.normal(k, shape, f32)
    col_scale = jnp.concatenate([
        jnp.ones((2 * D_FOX,), f32), jnp.full((D_FOX,), BETA, f32),
        jnp.ones((N_HEADS_FOX + 2 * D_DSA,), f32), jnp.full((D_DSA,), BETA, f32),
        jnp.ones((N_IDX_HEADS * IDX_DIM + IDX_DIM + N_IDX_HEADS,), f32)])
    return {
        'x_prompt': nrm(ks[0], (BATCH, SEQ, D_MODEL)),
        'x_sample': nrm(ks[1], (DEC_BATCH, DEC_SEQ, D_MODEL)),
        'p_prompt': nrm(ks[2], (DEPTH, BATCH, SEQ, PLE_DIM)),
        'p_sample': nrm(ks[3], (DEPTH, DEC_BATCH, DEC_SEQ, PLE_DIM)),
        'cache_fox_k': nrm(ks[4], (DEPTH, DEC_BATCH, PAST_LEN, N_HEADS_FOX, HEAD_DIM)),
        'cache_fox_v': BETA * nrm(ks[5], (DEPTH, DEC_BATCH, PAST_LEN, N_HEADS_FOX, HEAD_DIM)),
        'cache_fox_logf': jax.nn.log_sigmoid(3.0 + nrm(ks[6], (DEPTH, DEC_BATCH, PAST_LEN, N_HEADS_FOX))),
        'cache_dsa_k': nrm(ks[7], (DEPTH, DEC_BATCH, PAST_LEN, N_HEADS_DSA, HEAD_DIM)),
        'cache_dsa_v': BETA * nrm(ks[8], (DEPTH, DEC_BATCH, PAST_LEN, N_HEADS_DSA, HEAD_DIM)),
        'cache_idx_k': nrm(ks[9], (DEPTH, DEC_BATCH, PAST_LEN, IDX_DIM)),
        'w_in': nrm(ks[10], (DEPTH, D_MODEL, N_IN)) * D_MODEL ** -0.5 * col_scale,
        'b_f': 3.0 + 0.1 * nrm(ks[11], (DEPTH, N_HEADS_FOX)),
        'w_o': nrm(ks[12], (DEPTH, D_MIX, D_MODEL)) * BETA * D_MIX ** -0.5,
        'ln1_g': 1.0 + 0.02 * nrm(ks[13], (DEPTH, D_MODEL)),
        'ln1_b': 0.02 * nrm(ks[14], (DEPTH, D_MODEL)),
        'w_up': nrm(ks[15], (DEPTH, D_MODEL, D_FF)) * D_MODEL ** -0.5,
        'w_down': nrm(ks[16], (DEPTH, D_FF, D_MODEL)) * BETA * D_FF ** -0.5,
        'ln2_g': 1.0 + 0.02 * nrm(ks[17], (DEPTH, D_MODEL)),
        'ln2_b': 0.02 * nrm(ks[18], (DEPTH, D_MODEL)),
        'w_ple': nrm(ks[19], (DEPTH, PLE_DIM, D_MODEL)) * PLE_DIM ** -0.5,
        'w_ple_gate': nrm(ks[20], (DEPTH, D_MODEL, D_MODEL)) * D_MODEL ** -0.5,
        'b_ple_gate': 0.02 * nrm(ks[21], (DEPTH, D_MODEL)),
    }


def reference(x_prompt, x_sample, p_prompt, p_sample, cache_fox_k, cache_fox_v, cache_fox_logf,
              cache_dsa_k, cache_dsa_v, cache_idx_k, w_in, b_f, w_o, ln1_g, ln1_b, w_up, w_down,
              ln2_g, ln2_b, w_ple, w_ple_gate, b_ple_gate):
    y_p, y_s = x_prompt, x_sample
    new_rows = [[] for _ in range(12)]
    for i in range(DEPTH):
        mix_p, st_p = prompt_mixer(y_p, w_in[i], b_f[i])
        y_p = finish_layer(y_p, mix_p, p_prompt[i], w_o[i], ln1_g[i], ln1_b[i], w_up[i], w_down[i],
                           ln2_g[i], ln2_b[i], w_ple[i], w_ple_gate[i], b_ple_gate[i])
        mix_s, st_s = sample_mixer(y_s, w_in[i], b_f[i], cache_fox_k[i], cache_fox_v[i], cache_fox_logf[i],
                                   cache_dsa_k[i], cache_dsa_v[i], cache_idx_k[i])
        y_s = finish_layer(y_s, mix_s, p_sample[i], w_o[i], ln1_g[i], ln1_b[i], w_up[i], w_down[i],
                           ln2_g[i], ln2_b[i], w_ple[i], w_ple_gate[i], b_ple_gate[i])
        for lst, a in zip(new_rows, st_p + st_s):
            lst.append(a)
    (fox_k_p, fox_v_p, fox_logf_p, dsa_k_p, dsa_v_p, idx_k_p,
     fox_k_s, fox_v_s, fox_logf_s, dsa_k_s, dsa_v_s, idx_k_s) = [jnp.stack(a, axis=0) for a in new_rows]
    return (y_p, y_s, fox_k_p, fox_v_p, fox_logf_p, dsa_k_p, dsa_v_p, idx_k_p,
            fox_k_s, fox_v_s, fox_logf_s, dsa_k_s, dsa_v_s, idx_k_s)
```

```python
import functools

import jax
import jax.numpy as jnp
from jax import lax
from jax.experimental import pallas as pl
from jax.experimental.pallas import tpu as pltpu

F32 = jnp.float32
BF16 = jnp.bfloat16
I32 = jnp.int32

HEAD_DIM = 64
N_HEADS = 8
N_PAIRS = N_HEADS // 2
IDX_DIM = 32
N_IDX_HEADS = 8
CHUNK = 64
TOPK_MAX = 256
ROPE_THETA = 10000.0
LN_EPS = 1e-5
NEG = -1e30
LANES = 128
INT_MIN = -2 ** 31

D_HEADS = N_HEADS * HEAD_DIM
C_FQ, C_FK, C_FV, C_DQ, C_DK, C_DV = (i * D_HEADS for i in range(6))
C_IQ = 6 * D_HEADS
C_IKR = C_IQ + N_IDX_HEADS * IDX_DIM
C_MISC = C_IKR + N_IDX_HEADS * IDX_DIM
N_COLS = C_MISC + LANES

VMEM_LIMIT = 56 * 1024 * 1024


def _params(sem):
    return pltpu.CompilerParams(dimension_semantics=sem, vmem_limit_bytes=VMEM_LIMIT)


def _const_spec(shape):
    nd = len(shape)
    return pl.BlockSpec(shape, lambda *_: (0,) * nd, pipeline_mode=pl.Buffered(1))


def _rope_slab(t, cs, sn, first, half):
    sw = jnp.where(first, pltpu.roll(t, LANES - half, 1), pltpu.roll(t, half, 1))
    return t * cs + sw * sn


def _proj_kernel(x_ref, w_ref, c64_ref, s64_ref, c32_ref, s32_ref, bf_ref,
                 fq_o, fk_o, fv_o, dq_o, dk_o, dv_o, iq_o, ikr_o,
                 fk32_o, fv32_o, dk32_o, dv32_o, ik32_o, logf_o, iw_o):
    tm = x_ref.shape[0]
    xb = x_ref[...].astype(BF16)

    def mm(c0, n):
        return jnp.dot(xb, w_ref[:, c0:c0 + n], preferred_element_type=F32)

    lane = lax.broadcasted_iota(I32, (tm, LANES), 1)
    first64 = (lane % HEAD_DIM) < HEAD_DIM // 2
    first32 = (lane % IDX_DIM) < IDX_DIM // 2
    q_scale = HEAD_DIM ** -0.5

    fq_o[...] = (mm(C_FQ, D_HEADS) * q_scale).astype(BF16)
    fk = mm(C_FK, D_HEADS)
    fk32_o[...] = fk
    fk_o[...] = fk.astype(BF16)
    fv = mm(C_FV, D_HEADS)
    fv32_o[...] = fv
    fv_o[...] = fv.astype(BF16)

    c64 = c64_ref[...]
    s64 = s64_ref[...]
    dq = mm(C_DQ, D_HEADS)
    dk = mm(C_DK, D_HEADS)
    for j in range(D_HEADS // LANES):
        sl = slice(LANES * j, LANES * (j + 1))
        dq_o[:, sl] = (_rope_slab(dq[:, sl], c64, s64, first64, HEAD_DIM // 2) * q_scale).astype(BF16)
        r = _rope_slab(dk[:, sl], c64, s64, first64, HEAD_DIM // 2)
        dk32_o[:, sl] = r
        dk_o[:, sl] = r.astype(BF16)
    dv = mm(C_DV, D_HEADS)
    dv32_o[...] = dv
    dv_o[...] = dv.astype(BF16)

    c32 = c32_ref[...]
    s32 = s32_ref[...]
    w_idx = N_IDX_HEADS * IDX_DIM
    iq = mm(C_IQ, w_idx)
    ikr = mm(C_IKR, w_idx)
    for j in range(w_idx // LANES):
        sl = slice(LANES * j, LANES * (j + 1))
        iq_o[:, sl] = _rope_slab(iq[:, sl], c32, s32, first32, IDX_DIM // 2).astype(BF16)
        r = _rope_slab(ikr[:, sl], c32, s32, first32, IDX_DIM // 2)
        ikr_o[:, sl] = r.astype(BF16)
        if j == 0:
            ik32_o[...] = r[:, :IDX_DIM]

    misc = mm(C_MISC, LANES)
    iw_o[...] = misc[:, 0:N_IDX_HEADS] * (N_IDX_HEADS ** -0.5 * IDX_DIM ** -0.5)
    z = misc[:, N_IDX_HEADS:N_IDX_HEADS + N_HEADS] + bf_ref[...]
    logf_o[...] = jnp.minimum(z, 0.0) - jnp.log1p(jnp.exp(-jnp.abs(z)))


def _rope_tables(pos, dim):
    half = dim // 2
    inv_freq = ROPE_THETA ** (-jnp.arange(half, dtype=F32) / half)
    ang = pos.astype(F32)[:, None] * inv_freq[None, :]
    cos = jnp.cos(ang)
    sin = jnp.sin(ang)
    reps = LANES // dim
    cos_t = jnp.tile(jnp.concatenate([cos, cos], axis=1), (1, reps))
    sin_t = jnp.tile(jnp.concatenate([-sin, sin], axis=1), (1, reps))
    return cos_t, sin_t


def _project(x2d, w_r, b_f, pos, tm):
    m, d = x2d.shape
    n_pos = pos.shape[0]
    if n_pos < tm:
        pos = jnp.tile(pos, tm // n_pos)
        n_pos = tm
    n_pos_blocks = n_pos // tm
    c64, s64 = _rope_tables(pos, HEAD_DIM)
    c32, s32 = _rope_tables(pos, IDX_DIM)
    w_idx = N_IDX_HEADS * IDX_DIM

    row = lambda n: pl.BlockSpec((tm, n), lambda i: (i, 0))
    tab = pl.BlockSpec((tm, LANES), lambda i: (i % n_pos_blocks, 0))
    out_shape = (
        [jax.ShapeDtypeStruct((m, D_HEADS), BF16)] * 6
        + [jax.ShapeDtypeStruct((m, w_idx), BF16)] * 2
        + [jax.ShapeDtypeStruct((m, D_HEADS), F32)] * 4
        + [jax.ShapeDtypeStruct((m, IDX_DIM), F32),
           jax.ShapeDtypeStruct((m, N_HEADS), F32),
           jax.ShapeDtypeStruct((m, N_IDX_HEADS), F32)])
    out_specs = ([row(D_HEADS)] * 6 + [row(w_idx)] * 2 + [row(D_HEADS)] * 4
                 + [row(IDX_DIM), row(N_HEADS), row(N_IDX_HEADS)])
    return pl.pallas_call(
        _proj_kernel,
        grid=(m // tm,),
        in_specs=[row(d), _const_spec((d, N_COLS)), tab, tab, tab, tab, _const_spec((1, N_HEADS))],
        out_specs=out_specs,
        out_shape=out_shape,
        compiler_params=_params(("parallel",)),
        name="project",
    )(x2d, w_r, c64, s64, c32, s32, b_f.reshape(1, N_HEADS))


def _split3(x):
    hi = x.astype(BF16)
    r = x - hi.astype(F32)
    mid = r.astype(BF16)
    lo = (r - mid.astype(F32)).astype(BF16)
    return hi, mid, lo


def _cumsum_kernel(x_ref, o_ref, carry_ref):
    t = x_ref.shape[1]

    @pl.when(pl.program_id(1) == 0)
    def _():
        carry_ref[...] = jnp.zeros_like(carry_ref)

    tri = (lax.broadcasted_iota(I32, (t, t), 0) >= lax.broadcasted_iota(I32, (t, t), 1)).astype(BF16)
    hi, mid, lo = _split3(x_ref[0])
    c = (jnp.dot(tri, hi, preferred_element_type=F32)
         + jnp.dot(tri, mid, preferred_element_type=F32)
         + jnp.dot(tri, lo, preferred_element_type=F32))
    c = c + carry_ref[...]
    o_ref[0] = c
    carry_ref[...] = c[t - 1:t, :]


def _cumsum(x, t):
    b, l, h = x.shape
    return pl.pallas_call(
        _cumsum_kernel,
        grid=(b, l // t),
        in_specs=[pl.BlockSpec((1, t, h), lambda i, j: (i, j, 0))],
        out_specs=pl.BlockSpec((1, t, h), lambda i, j: (i, j, 0)),
        out_shape=jax.ShapeDtypeStruct((b, l, h), F32),
        scratch_shapes=[pltpu.VMEM((1, h), F32)],
        compiler_params=_params(("parallel", "arbitrary")),
        name="cumsum",
    )(x)


def _head_queries(q):
    tq = q.shape[0]
    low = lax.broadcasted_iota(I32, (tq, LANES), 1) < HEAD_DIM
    zero = jnp.zeros((tq, LANES), q.dtype)
    out = []
    for pair in range(N_PAIRS):
        slab = q[:, LANES * pair:LANES * (pair + 1)]
        out.append(jnp.where(low, slab, zero))
        out.append(jnp.where(low, zero, slab))
    return out


def _nt_dot(a, b):
    return lax.dot_general(a, b, (((1,), (1,)), ((), ())), preferred_element_type=F32)


def _online_update(h, s, vb, m_ref, l_ref, acc_ref):
    tk = s.shape[1]
    m_prev = m_ref[h]
    m_new = jnp.maximum(m_prev, jnp.max(s, axis=1, keepdims=True))
    p = jnp.exp(s - pltpu.repeat(m_new, tk // LANES, axis=1))
    alpha = jnp.exp(m_prev - m_new)
    l_ref[h] = alpha * l_ref[h] + jnp.sum(p, axis=1, keepdims=True)
    acc_ref[h] = alpha * acc_ref[h] + jnp.dot(p.astype(BF16), vb, preferred_element_type=F32)
    m_ref[h] = m_new


def _init_state(m_ref, l_ref, acc_ref):
    m_ref[...] = jnp.full(m_ref.shape, NEG, F32)
    l_ref[...] = jnp.zeros(l_ref.shape, F32)
    acc_ref[...] = jnp.zeros(acc_ref.shape, F32)


def _write_heads(o_ref, l_ref, acc_ref, col0):
    tq = acc_ref.shape[1]
    low = lax.broadcasted_iota(I32, (tq, LANES), 1) < HEAD_DIM
    for pair in range(N_PAIRS):
        o0 = acc_ref[2 * pair] / l_ref[2 * pair]
        o1 = acc_ref[2 * pair + 1] / l_ref[2 * pair + 1]
        o_ref[0, :, col0 + LANES * pair:col0 + LANES * (pair + 1)] = jnp.where(low, o0, o1).astype(o_ref.dtype)


def _fox_kernel(q_ref, k_ref, v_ref, cum_ref, cumt_ref, o_ref, m_ref, l_ref, acc_ref, *, tk):
    tq = q_ref.shape[1]
    i = pl.program_id(1)
    qh = _head_queries(q_ref[0])
    cum_q = cum_ref[0]
    cq = [cum_q[:, h:h + 1] for h in range(N_HEADS)]
    _init_state(m_ref, l_ref, acc_ref)
    row_pos = i * tq + lax.broadcasted_iota(I32, (tq, 1), 0)

    def block(j, masked):
        c0 = pl.multiple_of(j * tk, tk)
        if masked:
            keep = (c0 + lax.broadcasted_iota(I32, (1, tk), 1)) <= row_pos
        for pair in range(N_PAIRS):
            sl = slice(LANES * pair, LANES * (pair + 1))
            kb = k_ref[0, pl.ds(c0, tk), sl]
            vb = v_ref[0, pl.ds(c0, tk), sl]
            for h in (2 * pair, 2 * pair + 1):
                s = _nt_dot(qh[h], kb) + (cq[h] - cumt_ref[0, h:h + 1, pl.ds(c0, tk)])
                if masked:
                    s = jnp.where(keep, s, NEG)
                _online_update(h, s, vb, m_ref, l_ref, acc_ref)

    n_full = (i * tq) // tk

    def body(j, carry):
        block(j, False)
        return carry

    lax.fori_loop(0, n_full, body, 0)
    block(n_full, True)
    _write_heads(o_ref, l_ref, acc_ref, 0)


def _fox_prompt(q, k, v, cum, cumt, tq, tk):
    b, l, d = q.shape
    state = pltpu.VMEM((N_HEADS, tq, LANES), F32)
    return pl.pallas_call(
        functools.partial(_fox_kernel, tk=tk),
        grid=(b, l // tq),
        in_specs=[pl.BlockSpec((1, tq, d), lambda bi, i: (bi, i, 0)),
                  pl.BlockSpec((1, l, d), lambda bi, i: (bi, 0, 0)),
                  pl.BlockSpec((1, l, d), lambda bi, i: (bi, 0, 0)),
                  pl.BlockSpec((1, tq, N_HEADS), lambda bi, i: (bi, i, 0)),
                  pl.BlockSpec((1, N_HEADS, l), lambda bi, i: (bi, 0, 0))],
        out_specs=pl.BlockSpec((1, tq, d), lambda bi, i: (bi, i, 0)),
        out_shape=jax.ShapeDtypeStruct((b, l, d), BF16),
        scratch_shapes=[state, state, state],
        compiler_params=_params(("parallel", "arbitrary")),
        name="fox_prompt",
    )(q, k, v, cum, cumt)


def _sortable(x):
    b = lax.bitcast_convert_type(x + 0.0, I32)
    return b ^ ((b >> 31) & 0x7FFFFFFF)


def _lane_fold_count(mask):
    r, n = mask.shape
    ones = jnp.where(mask, 1, 0).astype(I32)
    acc = ones[:, 0:LANES]
    for c in range(1, n // LANES):
        acc = acc + ones[:, LANES * c:LANES * (c + 1)]
    return acc


def _kth_threshold(count_ge, rows, k):
    zero = jnp.zeros((rows, 1), I32)
    t0 = jnp.where(count_ge(zero) >= k, zero, jnp.full((rows, 1), INT_MIN, I32))

    def body(b, t):
        cand = t | jnp.left_shift(jnp.int32(1), 30 - b)
        return jnp.where(count_ge(cand) >= k, cand, t)

    return lax.fori_loop(0, 31, body, t0)


def _idx_head_queries(iq):
    lane_head = lax.broadcasted_iota(I32, iq.shape, 1) // IDX_DIM
    zero = jnp.zeros(iq.shape, iq.dtype)
    return [jnp.where(lane_head == h, iq, zero) for h in range(N_IDX_HEADS)]


def _dsa_kernel(iq_ref, ikr_ref, iw_ref, q_ref, k_ref, v_ref, o_ref,
                key_ref, m_ref, l_ref, acc_ref, *, tk, topk):
    tq = q_ref.shape[1]
    i = pl.program_id(1)
    n_blk = (i * tq) // tk + 1
    row_chunk = (i * tq + lax.broadcasted_iota(I32, (tq, 1), 0)) // CHUNK

    def col_chunk(c0):
        return (c0 + lax.broadcasted_iota(I32, (1, tk), 1)) // CHUNK

    iqh = _idx_head_queries(iq_ref[0])
    iw = iw_ref[0]
    wcol = [iw[:, h:h + 1] for h in range(N_IDX_HEADS)]

    def score_block(j, carry):
        c0 = pl.multiple_of(j * tk, tk)
        kb = ikr_ref[0, pl.ds(c0, tk), :]
        score = jnp.zeros((tq, tk), F32)
        for h in range(N_IDX_HEADS):
            score = score + wcol[h] * jnp.maximum(_nt_dot(iqh[h], kb), 0.0)
        score = jnp.where(col_chunk(c0) <= row_chunk, score, NEG)
        key_ref[:, pl.ds(c0, tk)] = _sortable(score)
        return carry

    lax.fori_loop(0, n_blk, score_block, 0)

    def count(pred):
        def body(j, acc):
            c0 = pl.multiple_of(j * tk, tk)
            return acc + _lane_fold_count(pred(key_ref[:, pl.ds(c0, tk)]))
        part = lax.fori_loop(0, n_blk, body, jnp.zeros((tq, LANES), I32))
        return jnp.sum(part, axis=1, keepdims=True)

    thr = _kth_threshold(lambda t: count(lambda kb: kb >= t), tq, topk)

    n_gt = count(lambda kb: kb > thr)
    n_ge = count(lambda kb: kb >= thr)
    neg_key = _sortable(jnp.full((1, 1), NEG, F32))
    overflow = jnp.logical_and(n_ge > topk, thr > neg_key)

    @pl.when(jnp.max(jnp.where(overflow, 1, 0)) > 0)
    def _():
        take = (topk - n_gt).astype(F32)
        strict_upper = (lax.broadcasted_iota(I32, (tk, tk), 0)
                        < lax.broadcasted_iota(I32, (tk, tk), 1)).astype(BF16)

        def body(j, seen):
            c0 = pl.multiple_of(j * tk, tk)
            kb = key_ref[:, pl.ds(c0, tk)]
            tie = kb == thr
            tie_b = jnp.where(tie, 1.0, 0.0).astype(BF16)
            rank = jnp.dot(tie_b, strict_upper, preferred_element_type=F32) + seen
            drop = jnp.logical_and(tie, rank >= take)
            key_ref[:, pl.ds(c0, tk)] = jnp.where(drop, INT_MIN, kb)
            return seen + jnp.sum(jnp.where(tie, 1.0, 0.0), axis=1, keepdims=True)

        lax.fori_loop(0, n_blk, body, jnp.zeros((tq, 1), F32))

    qh = _head_queries(q_ref[0])
    _init_state(m_ref, l_ref, acc_ref)

    def attend_block(j, carry):
        c0 = pl.multiple_of(j * tk, tk)
        kb_keys = key_ref[:, pl.ds(c0, tk)]
        admissible = col_chunk(c0) <= row_chunk
        bias = jnp.where(kb_keys >= thr, jnp.where(admissible, 0.0, NEG), NEG)
        for pair in range(N_PAIRS):
            sl = slice(LANES * pair, LANES * (pair + 1))
            kb = k_ref[0, pl.ds(c0, tk), sl]
            vb = v_ref[0, pl.ds(c0, tk), sl]
            for h in (2 * pair, 2 * pair + 1):
                _online_update(h, _nt_dot(qh[h], kb) + bias, vb, m_ref, l_ref, acc_ref)
        return carry

    lax.fori_loop(0, n_blk, attend_block, 0)
    _write_heads(o_ref, l_ref, acc_ref, 0)


def _dsa_prompt(iq, ikr, iw, q, k, v, tq, tk, topk):
    b, l, d = q.shape
    w_idx = iq.shape[2]
    state = pltpu.VMEM((N_HEADS, tq, LANES), F32)
    tile = lambda n: pl.BlockSpec((1, tq, n), lambda bi, i: (bi, i, 0))
    full = lambda n: pl.BlockSpec((1, l, n), lambda bi, i: (bi, 0, 0))
    return pl.pallas_call(
        functools.partial(_dsa_kernel, tk=tk, topk=topk),
        grid=(b, l // tq),
        in_specs=[tile(w_idx), full(w_idx), tile(N_IDX_HEADS), tile(d), full(d), full(d)],
        out_specs=tile(d),
        out_shape=jax.ShapeDtypeStruct((b, l, d), BF16),
        scratch_shapes=[pltpu.VMEM((tq, l), I32), state, state, state],
        compiler_params=_params(("parallel", "arbitrary")),
        name="dsa_prompt",
    )(iq, ikr, iw, q, k, v)


def _two_piece_attention(s_past, s_new, v_past, v_new):
    m = jnp.maximum(jnp.max(s_past, axis=1, keepdims=True), jnp.max(s_new, axis=1, keepdims=True))
    p_past = jnp.exp(s_past - m)
    p_new = jnp.exp(s_new - m)
    denom = jnp.sum(p_past, axis=1, keepdims=True) + jnp.sum(p_new, axis=1, keepdims=True)
    o = (jnp.dot(p_past.astype(BF16), v_past, preferred_element_type=F32)
         + jnp.dot(p_new.astype(BF16), v_new, preferred_element_type=F32))
    return o / denom


def _sample_kernel(fq_ref, fkn_ref, fvn_ref, dq_ref, dkn_ref, dvn_ref, iq_ref, ikrn_ref, iw_ref,
                   cfk_ref, cfv_ref, cdk_ref, cdv_ref, cikr_ref, cum_ref, cumt_ref,
                   o_ref, kp_ref, kn_ref, *, topk):
    t = fq_ref.shape[1]
    p_len = cfk_ref.shape[1]
    low = lax.broadcasted_iota(I32, (t, LANES), 1) < HEAD_DIM
    row = lax.broadcasted_iota(I32, (t, 1), 0)
    col_new = lax.broadcasted_iota(I32, (1, t), 1)

    qh = _head_queries(fq_ref[0])
    cum_q = cum_ref[0, p_len:p_len + t, :]
    causal_new = col_new <= row
    for pair in range(N_PAIRS):
        sl = slice(LANES * pair, LANES * (pair + 1))
        k_past = cfk_ref[0, :, sl].astype(BF16)
        v_past = cfv_ref[0, :, sl].astype(BF16)
        k_new = fkn_ref[0, :, sl]
        v_new = fvn_ref[0, :, sl]
        outs = []
        for h in (2 * pair, 2 * pair + 1):
            cq = cum_q[:, h:h + 1]
            s_past = _nt_dot(qh[h], k_past) + (cq - cumt_ref[0, h:h + 1, 0:p_len])
            s_new = _nt_dot(qh[h], k_new) + (cq - cumt_ref[0, h:h + 1, p_len:p_len + t])
            s_new = jnp.where(causal_new, s_new, NEG)
            outs.append(_two_piece_attention(s_past, s_new, v_past, v_new))
        o_ref[0, :, sl] = jnp.where(low, outs[0], outs[1]).astype(o_ref.dtype)

    iqh = _idx_head_queries(iq_ref[0])
    iw = iw_ref[0]
    ik_past = cikr_ref[0]
    ik_new = ikrn_ref[0]
    sc_past = jnp.zeros((t, p_len), F32)
    sc_new = jnp.zeros((t, t), F32)
    for h in range(N_IDX_HEADS):
        w = iw[:, h:h + 1]
        sc_past = sc_past + w * jnp.maximum(_nt_dot(iqh[h], ik_past), 0.0)
        sc_new = sc_new + w * jnp.maximum(_nt_dot(iqh[h], ik_new), 0.0)
    row_chunk = (p_len + row) // CHUNK
    adm_past = (lax.broadcasted_iota(I32, (1, p_len), 1) // CHUNK) <= row_chunk
    adm_new = ((p_len + col_new) // CHUNK) <= row_chunk
    kp_ref[...] = _sortable(jnp.where(adm_past, sc_past, NEG))
    kn_ref[...] = _sortable(jnp.where(adm_new, sc_new, NEG))

    def count(pred):
        part = _lane_fold_count(pred(kp_ref[...]))
        c = jnp.sum(part, axis=1, keepdims=True)
        return c + jnp.sum(jnp.where(pred(kn_ref[...]), 1, 0).astype(I32), axis=1, keepdims=True)

    thr = _kth_threshold(lambda x: count(lambda kb: kb >= x), t, topk)
    n_gt = count(lambda kb: kb > thr)
    n_ge = count(lambda kb: kb >= thr)
    neg_key = _sortable(jnp.full((1, 1), NEG, F32))
    overflow = jnp.logical_and(n_ge > topk, thr > neg_key)

    @pl.when(jnp.max(jnp.where(overflow, 1, 0)) > 0)
    def _():
        take = (topk - n_gt).astype(F32)
        seen = jnp.zeros((t, 1), F32)
        for ref, n in ((kp_ref, p_len), (kn_ref, t)):
            strict_upper = (lax.broadcasted_iota(I32, (n, n), 0)
                            < lax.broadcasted_iota(I32, (n, n), 1)).astype(BF16)
            kb = ref[...]
            tie = kb == thr
            rank = jnp.dot(jnp.where(tie, 1.0, 0.0).astype(BF16), strict_upper,
                           preferred_element_type=F32) + seen
            ref[...] = jnp.where(jnp.logical_and(tie, rank >= take), INT_MIN, kb)
            seen = seen + jnp.sum(jnp.where(tie, 1.0, 0.0), axis=1, keepdims=True)

    bias_past = jnp.where(kp_ref[...] >= thr, jnp.where(adm_past, 0.0, NEG), NEG)
    bias_new = jnp.where(kn_ref[...] >= thr, jnp.where(adm_new, 0.0, NEG), NEG)
    qd = _head_queries(dq_ref[0])
    for pair in range(N_PAIRS):
        sl = slice(LANES * pair, LANES * (pair + 1))
        k_past = cdk_ref[0, :, sl].astype(BF16)
        v_past = cdv_ref[0, :, sl].astype(BF16)
        k_new = dkn_ref[0, :, sl]
        v_new = dvn_ref[0, :, sl]
        outs = []
        for h in (2 * pair, 2 * pair + 1):
            s_past = _nt_dot(qd[h], k_past) + bias_past
            s_new = _nt_dot(qd[h], k_new) + bias_new
            outs.append(_two_piece_attention(s_past, s_new, v_past, v_new))
        o_ref[0, :, D_HEADS + LANES * pair:D_HEADS + LANES * (pair + 1)] = (
            jnp.where(low, outs[0], outs[1]).astype(o_ref.dtype))


def _sample_mixers(fq, fkn, fvn, dq, dkn, dvn, iq, ikrn, iw, cfk, cfv, cdk, cdv, cikr, cum, cumt, topk):
    b, t, d = fq.shape
    p_len = cfk.shape[1]
    w_idx = iq.shape[2]
    new = lambda n: pl.BlockSpec((1, t, n), lambda bi: (bi, 0, 0))
    past = lambda n: pl.BlockSpec((1, p_len, n), lambda bi: (bi, 0, 0))
    return pl.pallas_call(
        functools.partial(_sample_kernel, topk=topk),
        grid=(b,),
        in_specs=[new(d), new(d), new(d), new(d), new(d), new(d), new(w_idx), new(w_idx), new(N_IDX_HEADS),
                  past(d), past(d), past(d), past(d), past(w_idx),
                  pl.BlockSpec((1, p_len + t, N_HEADS), lambda bi: (bi, 0, 0)),
                  pl.BlockSpec((1, N_HEADS, p_len + t), lambda bi: (bi, 0, 0))],
        out_specs=pl.BlockSpec((1, t, 2 * d), lambda bi: (bi, 0, 0)),
        out_shape=jax.ShapeDtypeStruct((b, t, 2 * d), BF16),
        scratch_shapes=[pltpu.VMEM((t, p_len), I32), pltpu.VMEM((t, t), I32)],
        compiler_params=_params(("parallel",)),
        name="sample_mixers",
    )(fq, fkn, fvn, dq, dkn, dvn, iq, ikrn, iw, cfk, cfv, cdk, cdv, cikr, cum, cumt)


def _layer_norm(x, g, b):
    mu = jnp.mean(x, axis=-1, keepdims=True)
    xc = x - mu
    var = jnp.mean(xc * xc, axis=-1, keepdims=True)
    return xc * lax.rsqrt(var + LN_EPS) * g + b


def _finish_kernel(x_ref, mixf_ref, mixd_ref, p_ref, wo_ref, g1_ref, b1_ref, wup_ref, wdn_ref,
                   g2_ref, b2_ref, wple_ref, wg_ref, bg_ref, o_ref, *, alpha, ff_chunk):
    d_mix_half = mixf_ref.shape[1]
    a = (alpha * x_ref[...]
         + jnp.dot(mixf_ref[...], wo_ref[0:d_mix_half, :], preferred_element_type=F32)
         + jnp.dot(mixd_ref[...], wo_ref[d_mix_half:, :], preferred_element_type=F32))
    x1 = _layer_norm(a, g1_ref[...], b1_ref[...])
    x1b = x1.astype(BF16)
    ffn = jnp.zeros_like(x1)
    for c in range(wup_ref.shape[1] // ff_chunk):
        sl = slice(ff_chunk * c, ff_chunk * (c + 1))
        hid = jnp.maximum(jnp.dot(x1b, wup_ref[:, sl], preferred_element_type=F32), 0.0)
        ffn = ffn + jnp.dot((hid * hid).astype(BF16), wdn_ref[sl, :], preferred_element_type=F32)
    x2 = _layer_norm(alpha * x1 + ffn, g2_ref[...], b2_ref[...])
    gate = jax.nn.sigmoid(jnp.dot(x2.astype(BF16), wg_ref[...], preferred_element_type=F32) + bg_ref[...])
    pe = jnp.dot(p_ref[...].astype(BF16), wple_ref[...], preferred_element_type=F32)
    o_ref[...] = x2 + gate * pe


def _finish(x2d, mixes, p2d, w, alpha, tm):
    m, d = x2d.shape
    row = lambda n: pl.BlockSpec((tm, n), lambda i: (i, 0))
    if len(mixes) == 1:
        half = mixes[0].shape[1] // 2
        mix_specs = [pl.BlockSpec((tm, half), lambda i: (i, 0)), pl.BlockSpec((tm, half), lambda i: (i, 1))]
        mix_args = [mixes[0], mixes[0]]
    else:
        mix_specs = [row(mixes[0].shape[1]), row(mixes[1].shape[1])]
        mix_args = list(mixes)
    d_ff = w["w_up"].shape[1]
    vec = lambda n: _const_spec((1, n))
    return pl.pallas_call(
        functools.partial(_finish_kernel, alpha=alpha, ff_chunk=min(d_ff, 1024)),
        grid=(m // tm,),
        in_specs=[row(d)] + mix_specs + [row(p2d.shape[1]),
                  _const_spec(w["w_o"].shape), vec(d), vec(d),
                  _const_spec(w["w_up"].shape), _const_spec(w["w_down"].shape), vec(d), vec(d),
                  _const_spec(w["w_ple"].shape), _const_spec(w["w_ple_gate"].shape), vec(d)],
        out_specs=row(d),
        out_shape=jax.ShapeDtypeStruct((m, d), F32),
        compiler_params=_params(("parallel",)),
        name="finish",
    )(x2d, *mix_args, p2d, w["w_o"], w["ln1_g"], w["ln1_b"], w["w_up"], w["w_down"],
      w["ln2_g"], w["ln2_b"], w["w_ple"], w["w_ple_gate"], w["b_ple_gate"])


def _relayout_w_in(w_in):
    widths = (("fq", D_HEADS), ("fk", D_HEADS), ("fv", D_HEADS), ("fg", N_HEADS),
              ("dq", D_HEADS), ("dk", D_HEADS), ("dv", D_HEADS),
              ("iq", N_IDX_HEADS * IDX_DIM), ("ik", IDX_DIM), ("iw", N_IDX_HEADS))
    cols, o = {}, 0
    for name, n in widths:
        cols[name] = w_in[:, o:o + n]
        o += n
    pad = jnp.zeros((w_in.shape[0], LANES - N_IDX_HEADS - N_HEADS), w_in.dtype)
    w_r = jnp.concatenate([cols["fq"], cols["fk"], cols["fv"], cols["dq"], cols["dk"], cols["dv"],
                           cols["iq"], jnp.tile(cols["ik"], (1, N_IDX_HEADS)),
                           cols["iw"], cols["fg"], pad], axis=1)
    return w_r.astype(BF16)


def _layer(x_p, x_s, p_p, p_s, c_fk, c_fv, c_logf, c_dk, c_dv, c_ik, w, alpha):
    b, l, d = x_p.shape
    bs, t, _ = x_s.shape
    p_len = c_fk.shape[1]
    w_r = _relayout_w_in(w["w_in"])

    pp = _project(x_p.reshape(b * l, d), w_r, w["b_f"], jnp.arange(l, dtype=I32), tm=512)
    (fq, fk, fv, dq, dk, dv, iq, ikr, fk32, fv32, dk32, dv32, ik32, logf, iw) = pp
    r3 = lambda a: a.reshape(b, l, a.shape[-1])
    logf3 = r3(logf)
    cum = _cumsum(logf3, 512)
    cumt = jnp.swapaxes(cum, 1, 2)
    mix_fox = _fox_prompt(r3(fq), r3(fk), r3(fv), cum, cumt, tq=256, tk=512)
    mix_dsa = _dsa_prompt(r3(iq), r3(ikr), r3(iw), r3(dq), r3(dk), r3(dv), tq=256, tk=512,
                          topk=min(TOPK_MAX, l // 4))
    y_p = _finish(x_p.reshape(b * l, d), (mix_fox.reshape(b * l, -1), mix_dsa.reshape(b * l, -1)),
                  p_p.reshape(b * l, -1), w, alpha, tm=512).reshape(b, l, d)
    rows_p = (fk32.reshape(b, l, N_HEADS, HEAD_DIM), fv32.reshape(b, l, N_HEADS, HEAD_DIM), logf3,
              dk32.reshape(b, l, N_HEADS, HEAD_DIM), dv32.reshape(b, l, N_HEADS, HEAD_DIM),
              ik32.reshape(b, l, IDX_DIM))

    ps = _project(x_s.reshape(bs * t, d), w_r, w["b_f"], p_len + jnp.arange(t, dtype=I32), tm=512)
    (sfq, sfk, sfv, sdq, sdk, sdv, siq, sikr, sfk32, sfv32, sdk32, sdv32, sik32, slogf, siw) = ps
    s3 = lambda a: a.reshape(bs, t, a.shape[-1])
    slogf3 = s3(slogf)
    cum_s = _cumsum(jnp.concatenate([c_logf.astype(F32), slogf3], axis=1), p_len + t)
    cumt_s = jnp.swapaxes(cum_s, 1, 2)
    flat = lambda c: c.reshape(bs, p_len, -1)
    cikr = jnp.tile(c_ik.astype(BF16), (1, 1, N_IDX_HEADS))
    mix_s = _sample_mixers(s3(sfq), s3(sfk), s3(sfv), s3(sdq), s3(sdk), s3(sdv), s3(siq), s3(sikr), s3(siw),
                           flat(c_fk), flat(c_fv), flat(c_dk), flat(c_dv), cikr, cum_s, cumt_s,
                           topk=min(TOPK_MAX, (p_len + t) // 4))
    y_s = _finish(x_s.reshape(bs * t, d), (mix_s.reshape(bs * t, -1),), p_s.reshape(bs * t, -1),
                  w, alpha, tm=512).reshape(bs, t, d)
    rows_s = (sfk32.reshape(bs, t, N_HEADS, HEAD_DIM), sfv32.reshape(bs, t, N_HEADS, HEAD_DIM), slogf3,
              sdk32.reshape(bs, t, N_HEADS, HEAD_DIM), sdv32.reshape(bs, t, N_HEADS, HEAD_DIM),
              sik32.reshape(bs, t, IDX_DIM))
    return y_p, y_s, rows_p, rows_s


def kernel(x_prompt, x_sample, p_prompt, p_sample, cache_fox_k, cache_fox_v, cache_fox_logf, cache_dsa_k, cache_dsa_v, cache_idx_k, w_in, b_f, w_o, ln1_g, ln1_b, w_up, w_down, ln2_g, ln2_b, w_ple, w_ple_gate, b_ple_gate):
    depth = w_in.shape[0]
    alpha = (2 * depth) ** 0.25
    y_p, y_s = x_prompt, x_sample
    new_rows = [[] for _ in range(12)]
    for i in range(depth):
        vec = lambda a: a[i].reshape(1, -1)
        w = {"w_in": w_in[i], "b_f": b_f[i], "w_o": w_o[i].astype(BF16),
             "ln1_g": vec(ln1_g), "ln1_b": vec(ln1_b),
             "w_up": w_up[i].astype(BF16), "w_down": w_down[i].astype(BF16),
             "ln2_g": vec(ln2_g), "ln2_b": vec(ln2_b),
             "w_ple": w_ple[i].astype(BF16), "w_ple_gate": w_ple_gate[i].astype(BF16),
             "b_ple_gate": vec(b_ple_gate)}
        y_p, y_s, rows_p, rows_s = _layer(
            y_p, y_s, p_prompt[i], p_sample[i], cache_fox_k[i], cache_fox_v[i], cache_fox_logf[i],
            cache_dsa_k[i], cache_dsa_v[i], cache_idx_k[i], w, alpha)
        for lst, a in zip(new_rows, rows_p + rows_s):
            lst.append(a)
    return (y_p, y_s) + tuple(jnp.stack(a, axis=0) for a in new_rows)
```

```python
import functools

import jax
import jax.numpy as jnp
from jax import lax
from jax.experimental import pallas as pl
from jax.experimental.pallas import tpu as pltpu

F32 = jnp.float32
BF16 = jnp.bfloat16
I32 = jnp.int32

HEAD_DIM = 64
N_HEADS = 8
N_PAIRS = N_HEADS // 2
IDX_DIM = 32
N_IDX_HEADS = 8
CHUNK = 64
TOPK_MAX = 256
ROPE_THETA = 10000.0
LN_EPS = 1e-5
NEG = -1e30
LANES = 128
INT_MIN = -2 ** 31
LOG2E = 1.4426950408889634
VT_ROWS = 80
D_VT = N_HEADS * VT_ROWS

D_HEADS = N_HEADS * HEAD_DIM
C_FQ, C_FK, C_FV, C_DQ, C_DK, C_DV = (i * D_HEADS for i in range(6))
W_IDX = N_IDX_HEADS * IDX_DIM
C_IQ = 6 * D_HEADS
C_IKR = C_IQ + W_IDX
C_MISC = C_IKR + W_IDX
N_COLS = C_MISC + LANES
N_SPLIT = 3
L_IW = N_SPLIT * N_HEADS
L_ONE = N_SPLIT * N_HEADS

VMEM_LIMIT = 56 * 1024 * 1024


def _params(sem):
    return pltpu.CompilerParams(dimension_semantics=sem, vmem_limit_bytes=VMEM_LIMIT)


def _const_spec(shape):
    nd = len(shape)
    return pl.BlockSpec(shape, lambda *_: (0,) * nd, pipeline_mode=pl.Buffered(1))


def _split3(x):
    hi = x.astype(BF16)
    r = x - hi.astype(F32)
    mid = r.astype(BF16)
    lo = (r - mid.astype(F32)).astype(BF16)
    return hi, mid, lo


def _rope_slab(t, cs, sn, first, half):
    sw = jnp.where(first, pltpu.roll(t, LANES - half, 1), pltpu.roll(t, half, 1))
    return t * cs + sw * sn


def _store_heads(o_ref, val):
    for h in range(N_HEADS):
        o_ref[:, h, :] = val[:, HEAD_DIM * h:HEAD_DIM * (h + 1)]


def _proj_kernel(x_ref, w_ref, c64_ref, s64_ref, c32_ref, s32_ref, bf_ref, *o_refs, transposed):
    if transposed:
        (fk_o, dk_o, ikr_o, fk32_o, fv32_o, dk32_o, dv32_o, ik32_o, logf_o, logf3_o, iw_o,
         fqt_o, fvt_o, dqt_o, dvt_o, iqt_o) = o_refs
    else:
        (fk_o, dk_o, ikr_o, fk32_o, fv32_o, dk32_o, dv32_o, ik32_o, logf_o, logf3_o, iw_o,
         fq_o, fv_o, dq_o, dv_o, iq_o) = o_refs
    tm = x_ref.shape[0]
    xb = x_ref[...].astype(BF16)

    def mm(c0, n):
        return jnp.dot(xb, w_ref[:, c0:c0 + n], preferred_element_type=F32)

    def put(o, val):
        o[...] = (val.T if transposed else val).astype(BF16)

    def put_values(o, val):
        if not transposed:
            o[...] = val.astype(BF16)
            return
        vt = val.T
        ones = jnp.ones((VT_ROWS - HEAD_DIM, tm), F32)
        parts = []
        for h in range(N_HEADS):
            parts += [vt[HEAD_DIM * h:HEAD_DIM * (h + 1)], ones]
        o[...] = jnp.concatenate(parts, axis=0).astype(BF16)

    lane = lax.broadcasted_iota(I32, (tm, LANES), 1)
    first64 = (lane % HEAD_DIM) < HEAD_DIM // 2
    first32 = (lane % IDX_DIM) < IDX_DIM // 2
    q_scale = HEAD_DIM ** -0.5 * (LOG2E if transposed else 1.0)

    put(fqt_o if transposed else fq_o, mm(C_FQ, D_HEADS) * q_scale)
    fk = mm(C_FK, D_HEADS)
    _store_heads(fk32_o, fk)
    fk_o[...] = fk.astype(BF16)
    fv = mm(C_FV, D_HEADS)
    _store_heads(fv32_o, fv)
    put_values(fvt_o if transposed else fv_o, fv)

    c64 = c64_ref[...]
    s64 = s64_ref[...]
    dq = mm(C_DQ, D_HEADS)
    dk = mm(C_DK, D_HEADS)
    dq_r, dk_r = [], []
    for j in range(D_HEADS // LANES):
        sl = slice(LANES * j, LANES * (j + 1))
        dq_r.append(_rope_slab(dq[:, sl], c64, s64, first64, HEAD_DIM // 2) * q_scale)
        dk_r.append(_rope_slab(dk[:, sl], c64, s64, first64, HEAD_DIM // 2))
    put(dqt_o if transposed else dq_o, jnp.concatenate(dq_r, axis=1))
    dk_r = jnp.concatenate(dk_r, axis=1)
    _store_heads(dk32_o, dk_r)
    dk_o[...] = dk_r.astype(BF16)
    dv = mm(C_DV, D_HEADS)
    _store_heads(dv32_o, dv)
    put_values(dvt_o if transposed else dv_o, dv)

    c32 = c32_ref[...]
    s32 = s32_ref[...]
    iq = mm(C_IQ, W_IDX)
    ikr = mm(C_IKR, W_IDX)
    iq_r = []
    for j in range(W_IDX // LANES):
        sl = slice(LANES * j, LANES * (j + 1))
        iq_r.append(_rope_slab(iq[:, sl], c32, s32, first32, IDX_DIM // 2))
        r = _rope_slab(ikr[:, sl], c32, s32, first32, IDX_DIM // 2)
        ikr_o[:, sl] = r.astype(BF16)
        if j == 0:
            ik32_o[...] = r[:, :IDX_DIM]
    put(iqt_o if transposed else iq_o, jnp.concatenate(iq_r, axis=1))

    misc = mm(C_MISC, LANES)
    iw_o[...] = misc[:, L_IW:L_IW + N_IDX_HEADS] * (N_IDX_HEADS ** -0.5 * IDX_DIM ** -0.5)
    z = misc + bf_ref[...]
    logf = jnp.minimum(z, 0.0) - jnp.log1p(jnp.exp(-jnp.abs(z)))
    logf_o[...] = logf[:, 0:N_HEADS]
    logf3_o[...] = jnp.where(lane < N_SPLIT * N_HEADS, logf, 0.0)


def _rope_tables(pos, dim):
    half = dim // 2
    inv_freq = ROPE_THETA ** (-jnp.arange(half, dtype=F32) / half)
    ang = pos.astype(F32)[:, None] * inv_freq[None, :]
    cos = jnp.cos(ang)
    sin = jnp.sin(ang)
    reps = LANES // dim
    cos_t = jnp.tile(jnp.concatenate([cos, cos], axis=1), (1, reps))
    sin_t = jnp.tile(jnp.concatenate([-sin, sin], axis=1), (1, reps))
    return cos_t, sin_t


def _project(x2d, w_r, b_f, pos, tm, transposed):
    m, d = x2d.shape
    tm = min(tm, m)
    n_pos = pos.shape[0]
    if n_pos < tm:
        pos = jnp.tile(pos, tm // n_pos)
        n_pos = tm
    n_pos_blocks = n_pos // tm
    c64, s64 = _rope_tables(pos, HEAD_DIM)
    c32, s32 = _rope_tables(pos, IDX_DIM)
    bf128 = jnp.concatenate([jnp.tile(b_f.reshape(1, N_HEADS), (1, N_SPLIT)),
                             jnp.zeros((1, LANES - N_SPLIT * N_HEADS), F32)], axis=1)

    row = lambda n: pl.BlockSpec((tm, n), lambda i: (i, 0))
    heads = pl.BlockSpec((tm, N_HEADS, HEAD_DIM), lambda i: (i, 0, 0))
    col = lambda n: pl.BlockSpec((n, tm), lambda i: (0, i))
    tab = pl.BlockSpec((tm, LANES), lambda i: (i % n_pos_blocks, 0))
    sds = jax.ShapeDtypeStruct
    heads_sds = sds((m, N_HEADS, HEAD_DIM), F32)
    out_shape = [sds((m, D_HEADS), BF16), sds((m, D_HEADS), BF16), sds((m, W_IDX), BF16),
                 heads_sds, heads_sds, heads_sds, heads_sds,
                 sds((m, IDX_DIM), F32), sds((m, N_HEADS), F32), sds((m, LANES), F32),
                 sds((m, N_IDX_HEADS), F32)]
    out_specs = [row(D_HEADS), row(D_HEADS), row(W_IDX), heads, heads, heads, heads,
                 row(IDX_DIM), row(N_HEADS), row(LANES), row(N_IDX_HEADS)]
    for n, is_value in ((D_HEADS, False), (D_HEADS, True), (D_HEADS, False), (D_HEADS, True), (W_IDX, False)):
        if transposed:
            n = D_VT if is_value else n
            out_shape.append(sds((n, m), BF16))
            out_specs.append(col(n))
        else:
            out_shape.append(sds((m, n), BF16))
            out_specs.append(row(n))
    return pl.pallas_call(
        functools.partial(_proj_kernel, transposed=transposed),
        grid=(m // tm,),
        in_specs=[row(d), _const_spec((d, N_COLS)), tab, tab, tab, tab, _const_spec((1, LANES))],
        out_specs=out_specs,
        out_shape=out_shape,
        compiler_params=_params(("parallel",)),
        name="project",
    )(x2d, w_r, c64, s64, c32, s32, bf128)


def _cumsum_kernel(x_ref, cum_o, aux_o, carry_ref):
    t = x_ref.shape[1]

    @pl.when(pl.program_id(1) == 0)
    def _():
        carry_ref[...] = jnp.zeros_like(carry_ref)

    tri = (lax.broadcasted_iota(I32, (t, t), 0) >= lax.broadcasted_iota(I32, (t, t), 1)).astype(BF16)
    hi, mid, lo = _split3(x_ref[0])
    c = (jnp.dot(tri, hi, preferred_element_type=F32)
         + jnp.dot(tri, mid, preferred_element_type=F32)
         + jnp.dot(tri, lo, preferred_element_type=F32))
    c = c + carry_ref[...]
    carry_ref[...] = c[t - 1:t, :]
    cum_o[0] = c[:, 0:N_HEADS]
    n_hi, n_mid, n_lo = (piece.astype(F32) for piece in _split3(-c * LOG2E))
    lane = lax.broadcasted_iota(I32, (t, LANES), 1)
    aux = jnp.where(lane < N_HEADS, n_hi,
                    jnp.where(lane < 2 * N_HEADS, n_mid,
                              jnp.where(lane < L_ONE, n_lo,
                                        jnp.where(lane < L_ONE + N_SPLIT, 1.0, 0.0))))
    aux_o[0] = aux.astype(BF16)


def _cumsum(x, t):
    b, l, w = x.shape
    blk = lambda n: pl.BlockSpec((1, t, n), lambda i, j: (i, j, 0))
    return pl.pallas_call(
        _cumsum_kernel,
        grid=(b, l // t),
        in_specs=[blk(w)],
        out_specs=[blk(N_HEADS), blk(w)],
        out_shape=[jax.ShapeDtypeStruct((b, l, N_HEADS), F32), jax.ShapeDtypeStruct((b, l, w), BF16)],
        scratch_shapes=[pltpu.VMEM((1, w), F32)],
        compiler_params=_params(("parallel", "arbitrary")),
        name="cumsum",
    )(x)


def _sortable(x):
    b = lax.bitcast_convert_type(x + 0.0, I32)
    return b ^ ((b >> 31) & 0x7FFFFFFF)


def _kth_threshold(count_ge, n_total, shape, k):
    zero = jnp.zeros(shape, I32)
    c0 = count_ge(zero)
    ok0 = c0 >= k
    t0 = jnp.where(ok0, zero, jnp.full(shape, INT_MIN, I32))
    n0 = jnp.where(ok0, c0, n_total)

    def cond(st):
        b, _, n = st
        return jnp.logical_and(b < 31, jnp.max(jnp.where(n != k, 1, 0)) > 0)

    def body(st):
        b, t, n = st
        cand = t | jnp.left_shift(jnp.int32(1), 30 - b)
        c = count_ge(cand)
        ok = c >= k
        return b + 1, jnp.where(ok, cand, t), jnp.where(ok, c, n)

    _, t, n = lax.while_loop(cond, body, (jnp.int32(0), t0, n0))
    return t, n


def _init_state_t(tq):
    return jnp.full((N_HEADS, tq), NEG, F32), jnp.zeros((D_VT, tq), F32)


def _online_update_t(state, logits_and_values):
    m_all, acc_all = state
    pre = [logits_and_values(h) for h in range(N_HEADS)]
    m_out, acc_out = [], []
    for h, (s, vt) in enumerate(pre):
        m_prev = m_all[h:h + 1, :]
        m_new = jnp.maximum(m_prev, jnp.max(s, axis=0, keepdims=True))
        p = jnp.exp2(s - m_new).astype(BF16)
        m_out.append(m_new)
        acc_out.append(jnp.exp2(m_prev - m_new) * acc_all[VT_ROWS * h:VT_ROWS * (h + 1)]
                       + jnp.dot(vt, p, preferred_element_type=F32))
    return jnp.concatenate(m_out, axis=0), jnp.concatenate(acc_out, axis=0)


def _write_heads_t(o_ref, state):
    _, acc_all = state
    out_t = jnp.concatenate(
        [acc_all[VT_ROWS * h:VT_ROWS * h + HEAD_DIM] / acc_all[VT_ROWS * h + HEAD_DIM:VT_ROWS * h + HEAD_DIM + 1]
         for h in range(N_HEADS)], axis=0)
    o_ref[0] = out_t.T.astype(o_ref.dtype)


def _padded_head_queries_t(qt):
    zeros = jnp.zeros((HEAD_DIM, qt.shape[1]), qt.dtype)
    out = []
    for h in range(N_HEADS):
        qh = qt[HEAD_DIM * h:HEAD_DIM * (h + 1)]
        out.append(jnp.concatenate([qh, zeros] if h % 2 == 0 else [zeros, qh], axis=0))
    return out


def _fox_kernel(qt_ref, k_ref, aux_ref, vt_ref, cumt_ref, o_ref, *, tk):
    tq = qt_ref.shape[1]
    i = pl.program_id(1)
    cum_q = cumt_ref[0]
    row = lax.broadcasted_iota(I32, (LANES, tq), 0)
    rhs = []
    for h, qpad in enumerate(_padded_head_queries_t(qt_ref[...])):
        c_hi, c_mid, c_lo = (piece.astype(F32) for piece in _split3(cum_q[h:h + 1, :] * LOG2E))
        pick = jnp.where(row < L_ONE, jnp.where(row % N_HEADS == h, 1.0, 0.0), 0.0)
        sel = jnp.where(row == L_ONE, c_hi,
                        jnp.where(row == L_ONE + 1, c_mid, jnp.where(row == L_ONE + 2, c_lo, pick)))
        rhs.append(jnp.concatenate([qpad, sel.astype(BF16)], axis=0))
    q_pos = i * tq + lax.broadcasted_iota(I32, (1, tq), 1)

    def block(j, state, masked):
        c0 = pl.multiple_of(j * tk, tk)
        aux = aux_ref[0, pl.ds(c0, tk), :]
        if masked:
            keep = (c0 + lax.broadcasted_iota(I32, (tk, 1), 0)) <= q_pos

        def logits_and_values(h):
            pair = h // 2
            lhs = jnp.concatenate([k_ref[0, pl.ds(c0, tk), LANES * pair:LANES * (pair + 1)], aux], axis=1)
            s = jnp.dot(lhs, rhs[h], preferred_element_type=F32)
            if masked:
                s = jnp.where(keep, s, NEG)
            return s, vt_ref[VT_ROWS * h:VT_ROWS * (h + 1), pl.ds(c0, tk)]

        return _online_update_t(state, logits_and_values)

    n_full = (i * tq) // tk
    state = lax.fori_loop(0, n_full, lambda j, st: block(j, st, False), _init_state_t(tq))
    state = block(n_full, state, True)
    _write_heads_t(o_ref, state)


def _fox_prompt(qt, k, aux, vt, cumt, tq, tk):
    b, l, d = k.shape
    nq = l // tq
    return pl.pallas_call(
        functools.partial(_fox_kernel, tk=tk),
        grid=(b, nq),
        in_specs=[pl.BlockSpec((d, tq), lambda bi, i: (0, bi * nq + i)),
                  pl.BlockSpec((1, l, d), lambda bi, i: (bi, 0, 0)),
                  pl.BlockSpec((1, l, LANES), lambda bi, i: (bi, 0, 0)),
                  pl.BlockSpec((D_VT, l), lambda bi, i: (0, bi)),
                  pl.BlockSpec((1, N_HEADS, tq), lambda bi, i: (bi, 0, i))],
        out_specs=pl.BlockSpec((1, tq, d), lambda bi, i: (bi, i, 0)),
        out_shape=jax.ShapeDtypeStruct((b, l, d), BF16),
        compiler_params=_params(("parallel", "arbitrary")),
        name="fox_prompt",
    )(qt, k, aux, vt, cumt)


def _dsa_kernel(iqt_ref, ikr_ref, iwt_ref, qt_ref, k_ref, vt_ref, o_ref, key_ref, *, tk, topk):
    tq = qt_ref.shape[1]
    i = pl.program_id(1)
    n_blk = (i * tq) // tk + 1
    q_chunk = (i * tq + lax.broadcasted_iota(I32, (1, tq), 1)) // CHUNK

    def admissible(c0):
        return (c0 + lax.broadcasted_iota(I32, (tk, 1), 0)) // CHUNK <= q_chunk

    iqt = iqt_ref[...]
    iq_rhs = []
    for h in range(N_IDX_HEADS):
        parts = [jnp.zeros((IDX_DIM * h, tq), iqt.dtype), iqt[IDX_DIM * h:IDX_DIM * (h + 1)],
                 jnp.zeros((IDX_DIM * (N_IDX_HEADS - 1 - h), tq), iqt.dtype)]
        iq_rhs.append(jnp.concatenate([p for p in parts if p.shape[0]], axis=0))
    iw = iwt_ref[0]

    def score_block(j, carry):
        c0 = pl.multiple_of(j * tk, tk)
        kb = ikr_ref[0, pl.ds(c0, tk), :]
        score = jnp.zeros((tk, tq), F32)
        for h in range(N_IDX_HEADS):
            score = score + iw[h:h + 1, :] * jnp.maximum(
                jnp.dot(kb, iq_rhs[h], preferred_element_type=F32), 0.0)
        key_ref[pl.ds(c0, tk), :] = _sortable(jnp.where(admissible(c0), score, NEG))
        return carry

    lax.fori_loop(0, n_blk, score_block, 0)

    def count(pred):
        def body(j, acc):
            c0 = pl.multiple_of(j * tk, tk)
            hit = jnp.where(pred(key_ref[pl.ds(c0, tk), :]), 1, 0).astype(I32)
            return acc + jnp.sum(hit, axis=0, keepdims=True)
        return lax.fori_loop(0, n_blk, body, jnp.zeros((1, tq), I32))

    thr, n_ge = _kth_threshold(lambda t: count(lambda kb: kb >= t), n_blk * tk, (1, tq), topk)

    neg_key = _sortable(jnp.full((1, 1), NEG, F32))
    overflow = jnp.where(n_ge > topk, jnp.where(thr > neg_key, 1, 0), 0)

    @pl.when(jnp.max(overflow) > 0)
    def _():
        take = (topk - count(lambda kb: kb > thr)).astype(F32)
        strict_lower = (lax.broadcasted_iota(I32, (tk, tk), 0)
                        > lax.broadcasted_iota(I32, (tk, tk), 1)).astype(BF16)

        def body(j, seen):
            c0 = pl.multiple_of(j * tk, tk)
            kb = key_ref[pl.ds(c0, tk), :]
            tie = kb == thr
            tie_f = jnp.where(tie, 1.0, 0.0)
            rank = jnp.dot(strict_lower, tie_f.astype(BF16), preferred_element_type=F32) + seen
            drop = jnp.where(tie, jnp.where(rank >= take, 1, 0), 0)
            key_ref[pl.ds(c0, tk), :] = jnp.where(drop > 0, INT_MIN, kb)
            return seen + jnp.sum(tie_f, axis=0, keepdims=True)

        lax.fori_loop(0, n_blk, body, jnp.zeros((1, tq), F32))

    rhs = _padded_head_queries_t(qt_ref[...])

    def attend_block(j, state):
        c0 = pl.multiple_of(j * tk, tk)
        bias = jnp.where(key_ref[pl.ds(c0, tk), :] >= thr, jnp.where(admissible(c0), 0.0, NEG), NEG)

        def logits_and_values(h):
            pair = h // 2
            kb = k_ref[0, pl.ds(c0, tk), LANES * pair:LANES * (pair + 1)]
            s = jnp.dot(kb, rhs[h], preferred_element_type=F32) + bias
            return s, vt_ref[VT_ROWS * h:VT_ROWS * (h + 1), pl.ds(c0, tk)]

        return _online_update_t(state, logits_and_values)

    _write_heads_t(o_ref, lax.fori_loop(0, n_blk, attend_block, _init_state_t(tq)))


def _dsa_prompt(iqt, ikr, iwt, qt, k, vt, tq, tk, topk):
    b, l, d = k.shape
    nq = l // tq
    w_idx = ikr.shape[2]
    qcol = lambda n: pl.BlockSpec((n, tq), lambda bi, i: (0, bi * nq + i))
    full = lambda n: pl.BlockSpec((1, l, n), lambda bi, i: (bi, 0, 0))
    return pl.pallas_call(
        functools.partial(_dsa_kernel, tk=tk, topk=topk),
        grid=(b, nq),
        in_specs=[qcol(w_idx), full(w_idx), pl.BlockSpec((1, N_IDX_HEADS, tq), lambda bi, i: (bi, 0, i)),
                  qcol(d), full(d), pl.BlockSpec((D_VT, l), lambda bi, i: (0, bi))],
        out_specs=pl.BlockSpec((1, tq, d), lambda bi, i: (bi, i, 0)),
        out_shape=jax.ShapeDtypeStruct((b, l, d), BF16),
        scratch_shapes=[pltpu.VMEM((l, tq), I32)],
        compiler_params=_params(("parallel", "arbitrary")),
        name="dsa_prompt",
    )(iqt, ikr, iwt, qt, k, vt)


def _head_queries(q):
    t = q.shape[0]
    low = lax.broadcasted_iota(I32, (t, LANES), 1) < HEAD_DIM
    zero = jnp.zeros((t, LANES), q.dtype)
    out = []
    for pair in range(N_PAIRS):
        slab = q[:, LANES * pair:LANES * (pair + 1)]
        out.append(jnp.where(low, slab, zero))
        out.append(jnp.where(low, zero, slab))
    return out


def _idx_head_queries(iq):
    lane_head = lax.broadcasted_iota(I32, iq.shape, 1) // IDX_DIM
    zero = jnp.zeros(iq.shape, iq.dtype)
    return [jnp.where(lane_head == h, iq, zero) for h in range(N_IDX_HEADS)]


def _nt_dot(a, b):
    return lax.dot_general(a, b, (((1,), (1,)), ((), ())), preferred_element_type=F32)


def _lane_fold_count(mask):
    ones = jnp.where(mask, 1, 0).astype(I32)
    acc = ones[:, 0:LANES]
    for c in range(1, mask.shape[1] // LANES):
        acc = acc + ones[:, LANES * c:LANES * (c + 1)]
    return acc


def _two_piece_attention(s_past, s_new, v_past, v_new):
    m = jnp.maximum(jnp.max(s_past, axis=1, keepdims=True), jnp.max(s_new, axis=1, keepdims=True))
    p_past = jnp.exp(s_past - m)
    p_new = jnp.exp(s_new - m)
    denom = jnp.sum(p_past, axis=1, keepdims=True) + jnp.sum(p_new, axis=1, keepdims=True)
    o = (jnp.dot(p_past.astype(BF16), v_past, preferred_element_type=F32)
         + jnp.dot(p_new.astype(BF16), v_new, preferred_element_type=F32))
    return o / denom


def _sample_kernel(fq_ref, fkn_ref, fvn_ref, dq_ref, dkn_ref, dvn_ref, iq_ref, ikrn_ref, iw_ref,
                   cfk_ref, cfv_ref, cdk_ref, cdv_ref, cikr_ref, cum_ref, cumt_ref,
                   o_ref, kp_ref, kn_ref, *, topk):
    t = fq_ref.shape[1]
    p_len = cfk_ref.shape[1]
    low = lax.broadcasted_iota(I32, (t, LANES), 1) < HEAD_DIM
    row = lax.broadcasted_iota(I32, (t, 1), 0)
    col_new = lax.broadcasted_iota(I32, (1, t), 1)

    qh = _head_queries(fq_ref[0])
    cum_q = cum_ref[0, p_len:p_len + t, :]
    causal_new = col_new <= row
    for pair in range(N_PAIRS):
        sl = slice(LANES * pair, LANES * (pair + 1))
        k_past = cfk_ref[0, :, sl].astype(BF16)
        v_past = cfv_ref[0, :, sl].astype(BF16)
        k_new = fkn_ref[0, :, sl]
        v_new = fvn_ref[0, :, sl]
        outs = []
        for h in (2 * pair, 2 * pair + 1):
            cq = cum_q[:, h:h + 1]
            s_past = _nt_dot(qh[h], k_past) + (cq - cumt_ref[0, h:h + 1, 0:p_len])
            s_new = _nt_dot(qh[h], k_new) + (cq - cumt_ref[0, h:h + 1, p_len:p_len + t])
            s_new = jnp.where(causal_new, s_new, NEG)
            outs.append(_two_piece_attention(s_past, s_new, v_past, v_new))
        o_ref[0, :, sl] = jnp.where(low, outs[0], outs[1]).astype(o_ref.dtype)

    iqh = _idx_head_queries(iq_ref[0])
    iw = iw_ref[0]
    ik_past = cikr_ref[0]
    ik_new = ikrn_ref[0]
    sc_past = jnp.zeros((t, p_len), F32)
    sc_new = jnp.zeros((t, t), F32)
    for h in range(N_IDX_HEADS):
        w = iw[:, h:h + 1]
        sc_past = sc_past + w * jnp.maximum(_nt_dot(iqh[h], ik_past), 0.0)
        sc_new = sc_new + w * jnp.maximum(_nt_dot(iqh[h], ik_new), 0.0)
    row_chunk = (p_len + row) // CHUNK
    adm_past = (lax.broadcasted_iota(I32, (1, p_len), 1) // CHUNK) <= row_chunk
    adm_new = ((p_len + col_new) // CHUNK) <= row_chunk
    kp_ref[...] = _sortable(jnp.where(adm_past, sc_past, NEG))
    kn_ref[...] = _sortable(jnp.where(adm_new, sc_new, NEG))

    def count(pred):
        c = jnp.sum(_lane_fold_count(pred(kp_ref[...])), axis=1, keepdims=True)
        return c + jnp.sum(jnp.where(pred(kn_ref[...]), 1, 0).astype(I32), axis=1, keepdims=True)

    thr, n_ge = _kth_threshold(lambda x: count(lambda kb: kb >= x), p_len + t, (t, 1), topk)
    neg_key = _sortable(jnp.full((1, 1), NEG, F32))
    overflow = jnp.where(n_ge > topk, jnp.where(thr > neg_key, 1, 0), 0)

    @pl.when(jnp.max(overflow) > 0)
    def _():
        take = (topk - count(lambda kb: kb > thr)).astype(F32)
        seen = jnp.zeros((t, 1), F32)
        for ref, n in ((kp_ref, p_len), (kn_ref, t)):
            strict_upper = (lax.broadcasted_iota(I32, (n, n), 0)
                            < lax.broadcasted_iota(I32, (n, n), 1)).astype(BF16)
            kb = ref[...]
            tie = kb == thr
            tie_f = jnp.where(tie, 1.0, 0.0)
            rank = jnp.dot(tie_f.astype(BF16), strict_upper, preferred_element_type=F32) + seen
            drop = jnp.where(tie, jnp.where(rank >= take, 1, 0), 0)
            ref[...] = jnp.where(drop > 0, INT_MIN, kb)
            seen = seen + jnp.sum(tie_f, axis=1, keepdims=True)

    bias_past = jnp.where(kp_ref[...] >= thr, jnp.where(adm_past, 0.0, NEG), NEG)
    bias_new = jnp.where(kn_ref[...] >= thr, jnp.where(adm_new, 0.0, NEG), NEG)
    qd = _head_queries(dq_ref[0])
    for pair in range(N_PAIRS):
        sl = slice(LANES * pair, LANES * (pair + 1))
        k_past = cdk_ref[0, :, sl].astype(BF16)
        v_past = cdv_ref[0, :, sl].astype(BF16)
        k_new = dkn_ref[0, :, sl]
        v_new = dvn_ref[0, :, sl]
        outs = []
        for h in (2 * pair, 2 * pair + 1):
            s_past = _nt_dot(qd[h], k_past) + bias_past
            s_new = _nt_dot(qd[h], k_new) + bias_new
            outs.append(_two_piece_attention(s_past, s_new, v_past, v_new))
        o_ref[0, :, D_HEADS + LANES * pair:D_HEADS + LANES * (pair + 1)] = (
            jnp.where(low, outs[0], outs[1]).astype(o_ref.dtype))


def _sample_mixers(fq, fkn, fvn, dq, dkn, dvn, iq, ikrn, iw, cfk, cfv, cdk, cdv, cikr, cum, cumt, topk):
    b, t, d = fq.shape
    p_len = cfk.shape[1]
    w_idx = iq.shape[2]
    new = lambda n: pl.BlockSpec((1, t, n), lambda bi: (bi, 0, 0))
    past = lambda n: pl.BlockSpec((1, p_len, n), lambda bi: (bi, 0, 0))
    return pl.pallas_call(
        functools.partial(_sample_kernel, topk=topk),
        grid=(b,),
        in_specs=[new(d), new(d), new(d), new(d), new(d), new(d), new(w_idx), new(w_idx), new(N_IDX_HEADS),
                  past(d), past(d), past(d), past(d), past(w_idx),
                  pl.BlockSpec((1, p_len + t, N_HEADS), lambda bi: (bi, 0, 0)),
                  pl.BlockSpec((1, N_HEADS, p_len + t), lambda bi: (bi, 0, 0))],
        out_specs=pl.BlockSpec((1, t, 2 * d), lambda bi: (bi, 0, 0)),
        out_shape=jax.ShapeDtypeStruct((b, t, 2 * d), BF16),
        scratch_shapes=[pltpu.VMEM((t, p_len), I32), pltpu.VMEM((t, t), I32)],
        compiler_params=_params(("parallel",)),
        name="sample_mixers",
    )(fq, fkn, fvn, dq, dkn, dvn, iq, ikrn, iw, cfk, cfv, cdk, cdv, cikr, cum, cumt)


def _layer_norm(x, g, b):
    mu = jnp.mean(x, axis=-1, keepdims=True)
    xc = x - mu
    var = jnp.mean(xc * xc, axis=-1, keepdims=True)
    return xc * lax.rsqrt(var + LN_EPS) * g + b


def _finish_kernel(x_ref, mixf_ref, mixd_ref, p_ref, wo_ref, g1_ref, b1_ref, wup_ref, wdn_ref,
                   g2_ref, b2_ref, wple_ref, wg_ref, bg_ref, o_ref, *, alpha, ff_chunk):
    d_mix_half = mixf_ref.shape[1]
    a = (alpha * x_ref[...]
         + jnp.dot(mixf_ref[...], wo_ref[0:d_mix_half, :], preferred_element_type=F32)
         + jnp.dot(mixd_ref[...], wo_ref[d_mix_half:, :], preferred_element_type=F32))
    x1 = _layer_norm(a, g1_ref[...], b1_ref[...])
    x1b = x1.astype(BF16)
    ffn = jnp.zeros_like(x1)
    for c in range(wup_ref.shape[1] // ff_chunk):
        sl = slice(ff_chunk * c, ff_chunk * (c + 1))
        hid = jnp.maximum(jnp.dot(x1b, wup_ref[:, sl], preferred_element_type=F32), 0.0)
        ffn = ffn + jnp.dot((hid * hid).astype(BF16), wdn_ref[sl, :], preferred_element_type=F32)
    x2 = _layer_norm(alpha * x1 + ffn, g2_ref[...], b2_ref[...])
    gate = jax.nn.sigmoid(jnp.dot(x2.astype(BF16), wg_ref[...], preferred_element_type=F32) + bg_ref[...])
    pe = jnp.dot(p_ref[...].astype(BF16), wple_ref[...], preferred_element_type=F32)
    o_ref[...] = x2 + gate * pe


def _finish(x2d, mixes, p2d, w, alpha, tm):
    m, d = x2d.shape
    tm = min(tm, m)
    row = lambda n: pl.BlockSpec((tm, n), lambda i: (i, 0))
    if len(mixes) == 1:
        half = mixes[0].shape[1] // 2
        mix_specs = [pl.BlockSpec((tm, half), lambda i: (i, 0)), pl.BlockSpec((tm, half), lambda i: (i, 1))]
        mix_args = [mixes[0], mixes[0]]
    else:
        mix_specs = [row(mixes[0].shape[1]), row(mixes[1].shape[1])]
        mix_args = list(mixes)
    d_ff = w["w_up"].shape[1]
    vec = lambda n: _const_spec((1, n))
    return pl.pallas_call(
        functools.partial(_finish_kernel, alpha=alpha, ff_chunk=min(d_ff, 1024)),
        grid=(m // tm,),
        in_specs=[row(d)] + mix_specs + [row(p2d.shape[1]),
                  _const_spec(w["w_o"].shape), vec(d), vec(d),
                  _const_spec(w["w_up"].shape), _const_spec(w["w_down"].shape), vec(d), vec(d),
                  _const_spec(w["w_ple"].shape), _const_spec(w["w_ple_gate"].shape), vec(d)],
        out_specs=row(d),
        out_shape=jax.ShapeDtypeStruct((m, d), F32),
        compiler_params=_params(("parallel",)),
        name="finish",
    )(x2d, *mix_args, p2d, w["w_o"], w["ln1_g"], w["ln1_b"], w["w_up"], w["w_down"],
      w["ln2_g"], w["ln2_b"], w["w_ple"], w["w_ple_gate"], w["b_ple_gate"])


def _relayout_w_in(w_in):
    widths = (("fq", D_HEADS), ("fk", D_HEADS), ("fv", D_HEADS), ("fg", N_HEADS),
              ("dq", D_HEADS), ("dk", D_HEADS), ("dv", D_HEADS),
              ("iq", W_IDX), ("ik", IDX_DIM), ("iw", N_IDX_HEADS))
    cols, o = {}, 0
    for name, n in widths:
        cols[name] = w_in[:, o:o + n]
        o += n
    pad = jnp.zeros((w_in.shape[0], LANES - N_SPLIT * N_HEADS - N_IDX_HEADS), w_in.dtype)
    w_r = jnp.concatenate([cols["fq"], cols["fk"], cols["fv"], cols["dq"], cols["dk"], cols["dv"],
                           cols["iq"], jnp.tile(cols["ik"], (1, N_IDX_HEADS)),
                           jnp.tile(cols["fg"], (1, N_SPLIT)), cols["iw"], pad], axis=1)
    return w_r.astype(BF16)


def _layer(x_p, x_s, p_p, p_s, c_fk, c_fv, c_logf, c_dk, c_dv, c_ik, w, alpha):
    b, l, d = x_p.shape
    bs, t, _ = x_s.shape
    p_len = c_fk.shape[1]
    w_r = _relayout_w_in(w["w_in"])

    (fk, dk, ikr, fk32, fv32, dk32, dv32, ik32, logf, logf3, iw, fqt, fvt, dqt, dvt, iqt) = _project(
        x_p.reshape(b * l, d), w_r, w["b_f"], jnp.arange(l, dtype=I32), tm=512, transposed=True)
    r3 = lambda a: a.reshape((b, l) + a.shape[1:])
    cum, aux = _cumsum(r3(logf3), 512)
    mix_fox = _fox_prompt(fqt, r3(fk), aux, fvt, jnp.swapaxes(cum, 1, 2), tq=256, tk=512)
    mix_dsa = _dsa_prompt(iqt, r3(ikr), jnp.swapaxes(r3(iw), 1, 2), dqt, r3(dk), dvt, tq=256, tk=512,
                          topk=min(TOPK_MAX, l // 4))
    y_p = _finish(x_p.reshape(b * l, d), (mix_fox.reshape(b * l, -1), mix_dsa.reshape(b * l, -1)),
                  p_p.reshape(b * l, -1), w, alpha, tm=512).reshape(b, l, d)
    rows_p = tuple(r3(a) for a in (fk32, fv32, logf, dk32, dv32, ik32))

    (sfk, sdk, sikr, sfk32, sfv32, sdk32, sdv32, sik32, slogf, slogf3, siw, sfq, sfv, sdq, sdv, siq) = _project(
        x_s.reshape(bs * t, d), w_r, w["b_f"], p_len + jnp.arange(t, dtype=I32), tm=512, transposed=False)
    s3 = lambda a: a.reshape((bs, t) + a.shape[1:])
    c_logf3 = jnp.pad(jnp.tile(c_logf.astype(F32), (1, 1, N_SPLIT)),
                      ((0, 0), (0, 0), (0, LANES - N_SPLIT * N_HEADS)))
    cum_s, _ = _cumsum(jnp.concatenate([c_logf3, s3(slogf3)], axis=1), p_len + t)
    flat = lambda c: c.reshape(bs, p_len, -1)
    cikr = jnp.tile(c_ik.astype(BF16), (1, 1, N_IDX_HEADS))
    mix_s = _sample_mixers(s3(sfq), s3(sfk), s3(sfv), s3(sdq), s3(sdk), s3(sdv), s3(siq), s3(sikr), s3(siw),
                           flat(c_fk), flat(c_fv), flat(c_dk), flat(c_dv), cikr, cum_s,
                           jnp.swapaxes(cum_s, 1, 2), topk=min(TOPK_MAX, (p_len + t) // 4))
    y_s = _finish(x_s.reshape(bs * t, d), (mix_s.reshape(bs * t, -1),), p_s.reshape(bs * t, -1),
                  w, alpha, tm=512).reshape(bs, t, d)
    rows_s = tuple(s3(a) for a in (sfk32, sfv32, slogf, sdk32, sdv32, sik32))
    return y_p, y_s, rows_p, rows_s


def kernel(x_prompt, x_sample, p_prompt, p_sample, cache_fox_k, cache_fox_v, cache_fox_logf, cache_dsa_k, cache_dsa_v, cache_idx_k, w_in, b_f, w_o, ln1_g, ln1_b, w_up, w_down, ln2_g, ln2_b, w_ple, w_ple_gate, b_ple_gate):
    depth = w_in.shape[0]
    alpha = (2 * depth) ** 0.25
    y_p, y_s = x_prompt, x_sample
    new_rows = [[] for _ in range(12)]
    for i in range(depth):
        vec = lambda a: a[i].reshape(1, -1)
        w = {"w_in": w_in[i], "b_f": b_f[i], "w_o": w_o[i].astype(BF16),
             "ln1_g": vec(ln1_g), "ln1_b": vec(ln1_b),
             "w_up": w_up[i].astype(BF16), "w_down": w_down[i].astype(BF16),
             "ln2_g": vec(ln2_g), "ln2_b": vec(ln2_b),
             "w_ple": w_ple[i].astype(BF16), "w_ple_gate": w_ple_gate[i].astype(BF16),
             "b_ple_gate": vec(b_ple_gate)}
        y_p, y_s, rows_p, rows_s = _layer(
            y_p, y_s, p_prompt[i], p_sample[i], cache_fox_k[i], cache_fox_v[i], cache_fox_logf[i],
            cache_dsa_k[i], cache_dsa_v[i], cache_idx_k[i], w, alpha)
        for lst, a in zip(new_rows, rows_p + rows_s):
            lst.append(a)
    return (y_p, y_s) + tuple(jnp.stack(a, axis=0) for a in new_rows)
```

```python
import functools

import jax
import jax.numpy as jnp
from jax import lax
from jax.experimental import pallas as pl
from jax.experimental.pallas import tpu as pltpu

F32 = jnp.float32
BF16 = jnp.bfloat16
I32 = jnp.int32

HEAD_DIM = 64
N_HEADS = 8
N_PAIRS = N_HEADS // 2
IDX_DIM = 32
N_IDX_HEADS = 8
CHUNK = 64
TOPK_MAX = 256
ROPE_THETA = 10000.0
LN_EPS = 1e-5
NEG = -1e30
LANES = 128
SUBLANES = 8
INT_MIN = -2 ** 31
BITS_PER_CHECK = 4
LOG2E = 1.4426950408889634
VT_ROWS = 80
D_VT = N_HEADS * VT_ROWS

D_HEADS = N_HEADS * HEAD_DIM
C_FQ, C_FK, C_FV, C_DQ, C_DK, C_DV = (i * D_HEADS for i in range(6))
W_IDX = N_IDX_HEADS * IDX_DIM
C_IQ = 6 * D_HEADS
C_IKR = C_IQ + W_IDX
C_MISC = C_IKR + W_IDX
N_COLS = C_MISC + LANES
N_SPLIT = 3
L_IW = N_SPLIT * N_HEADS
L_ONE = N_SPLIT * N_HEADS

VMEM_LIMIT = 56 * 1024 * 1024


def _params(sem):
    return pltpu.CompilerParams(dimension_semantics=sem, vmem_limit_bytes=VMEM_LIMIT)


def _const_spec(shape):
    nd = len(shape)
    return pl.BlockSpec(shape, lambda *_: (0,) * nd, pipeline_mode=pl.Buffered(1))


def _split3(x):
    hi = x.astype(BF16)
    r = x - hi.astype(F32)
    mid = r.astype(BF16)
    lo = (r - mid.astype(F32)).astype(BF16)
    return hi, mid, lo


def _dot3(a, b, a_is_f32):
    if a_is_f32:
        return sum(jnp.dot(p, b, preferred_element_type=F32) for p in _split3(a))
    return sum(jnp.dot(a, p, preferred_element_type=F32) for p in _split3(b))


def _tri(n, upper):
    r = lax.broadcasted_iota(I32, (n, n), 0)
    c = lax.broadcasted_iota(I32, (n, n), 1)
    return (r <= c if upper else r >= c).astype(BF16)


def _rope_slab(t, cs, sn, first, half):
    sw = jnp.where(first, pltpu.roll(t, LANES - half, 1), pltpu.roll(t, half, 1))
    return t * cs + sw * sn


def _rope(val, cs, sn, first, half):
    return jnp.concatenate([_rope_slab(val[:, LANES * j:LANES * (j + 1)], cs, sn, first, half)
                            for j in range(val.shape[1] // LANES)], axis=1)


def _with_ones_rows(vt, tm):
    ones = jnp.ones((VT_ROWS - HEAD_DIM, tm), F32)
    parts = []
    for h in range(N_HEADS):
        parts += [vt[HEAD_DIM * h:HEAD_DIM * (h + 1)], ones]
    return jnp.concatenate(parts, axis=0)


def _proj_core(x_ref, w_ref, c64_ref, s64_ref, c32_ref, s32_ref, bf_ref, q_scale):
    tm = x_ref.shape[0]
    xb = x_ref[...].astype(BF16)

    def mm(c0, n):
        return jnp.dot(xb, w_ref[:, c0:c0 + n], preferred_element_type=F32)

    lane = lax.broadcasted_iota(I32, (tm, LANES), 1)
    first64 = (lane % HEAD_DIM) < HEAD_DIM // 2
    first32 = (lane % IDX_DIM) < IDX_DIM // 2
    c64, s64, c32, s32 = c64_ref[...], s64_ref[...], c32_ref[...], s32_ref[...]
    misc = mm(C_MISC, LANES)
    z = misc + bf_ref[...]
    logf = jnp.minimum(z, 0.0) - jnp.log1p(jnp.exp(-jnp.abs(z)))
    misc = jnp.where(lane < L_IW, logf,
                     jnp.where(lane < L_IW + N_IDX_HEADS, misc * (N_IDX_HEADS ** -0.5 * IDX_DIM ** -0.5), 0.0))
    return dict(
        fq=mm(C_FQ, D_HEADS) * q_scale, fk=mm(C_FK, D_HEADS), fv=mm(C_FV, D_HEADS),
        dq=_rope(mm(C_DQ, D_HEADS), c64, s64, first64, HEAD_DIM // 2) * q_scale,
        dk=_rope(mm(C_DK, D_HEADS), c64, s64, first64, HEAD_DIM // 2), dv=mm(C_DV, D_HEADS),
        iq=_rope(mm(C_IQ, W_IDX), c32, s32, first32, IDX_DIM // 2),
        ikr=_rope(mm(C_IKR, W_IDX), c32, s32, first32, IDX_DIM // 2),
        misc=misc, lane=lane)


def _proj_prompt_kernel(x_ref, w_ref, c64_ref, s64_ref, c32_ref, s32_ref, bf_ref,
                        fk_o, dk_o, ikr_o, logf3_o, fqt_o, dqt_o, iqt_o, fvt_o, dvt_o,
                        fkt32_o, fvt32_o, dkt32_o, dvt32_o, ikt32_o, logft_o, iwt_o):
    tm = x_ref.shape[0]
    r = _proj_core(x_ref, w_ref, c64_ref, s64_ref, c32_ref, s32_ref, bf_ref, HEAD_DIM ** -0.5 * LOG2E)
    fk_o[...] = r["fk"].astype(BF16)
    dk_o[...] = r["dk"].astype(BF16)
    ikr_o[...] = r["ikr"].astype(BF16)
    logf3_o[...] = jnp.where(r["lane"] < L_IW, r["misc"], 0.0)
    fqt_o[...] = r["fq"].T.astype(BF16)
    dqt_o[...] = r["dq"].T.astype(BF16)
    iqt_o[...] = r["iq"].T.astype(BF16)
    for name, o32, obf in (("fv", fvt32_o, fvt_o), ("dv", dvt32_o, dvt_o)):
        vt = r[name].T
        o32[0] = vt
        obf[...] = _with_ones_rows(vt, tm).astype(BF16)
    fkt32_o[0] = r["fk"].T
    dkt32_o[0] = r["dk"].T
    ikt32_o[0] = r["ikr"][:, 0:LANES].T[0:IDX_DIM]
    misc_t = r["misc"].T
    logft_o[0] = misc_t[0:N_HEADS]
    iwt_o[0] = misc_t[L_IW:L_IW + N_IDX_HEADS]


def _store_heads(o_ref, val):
    tm = val.shape[0]
    for h in range(N_HEADS):
        o_ref[pl.ds(h, tm, stride=N_HEADS), :] = val[:, HEAD_DIM * h:HEAD_DIM * (h + 1)]


def _proj_sample_kernel(x_ref, w_ref, c64_ref, s64_ref, c32_ref, s32_ref, bf_ref,
                        fq_o, fk_o, fv_o, dq_o, dk_o, dv_o, iq_o, ikr_o, misc_o,
                        fk32_o, fv32_o, dk32_o, dv32_o, ik32_o, logf_o):
    r = _proj_core(x_ref, w_ref, c64_ref, s64_ref, c32_ref, s32_ref, bf_ref, HEAD_DIM ** -0.5)
    for name, o in (("fq", fq_o), ("fk", fk_o), ("fv", fv_o), ("dq", dq_o), ("dk", dk_o), ("dv", dv_o),
                    ("iq", iq_o), ("ikr", ikr_o)):
        o[...] = r[name].astype(BF16)
    misc_o[...] = r["misc"]
    for name, o in (("fk", fk32_o), ("fv", fv32_o), ("dk", dk32_o), ("dv", dv32_o)):
        _store_heads(o, r[name])
    ik32_o[...] = r["ikr"][:, 0:IDX_DIM]
    logf_o[...] = r["misc"][:, 0:N_HEADS]


def _rope_tables(pos, dim):
    half = dim // 2
    inv_freq = ROPE_THETA ** (-jnp.arange(half, dtype=F32) / half)
    ang = pos.astype(F32)[:, None] * inv_freq[None, :]
    cos = jnp.cos(ang)
    sin = jnp.sin(ang)
    reps = LANES // dim
    cos_t = jnp.tile(jnp.concatenate([cos, cos], axis=1), (1, reps))
    sin_t = jnp.tile(jnp.concatenate([-sin, sin], axis=1), (1, reps))
    return cos_t, sin_t


def _project(x2d, w_r, b_f, pos, tm, n_seq):
    m, d = x2d.shape
    tm = min(tm, m)
    n_pos = pos.shape[0]
    seq_len = n_pos
    if n_pos < tm:
        pos = jnp.tile(pos, tm // n_pos)
        n_pos = tm
    n_pos_blocks = n_pos // tm
    c64, s64 = _rope_tables(pos, HEAD_DIM)
    c32, s32 = _rope_tables(pos, IDX_DIM)
    bf128 = jnp.concatenate([jnp.tile(b_f.reshape(1, N_HEADS), (1, N_SPLIT)),
                             jnp.zeros((1, LANES - N_SPLIT * N_HEADS), F32)], axis=1)

    sds = jax.ShapeDtypeStruct
    row = lambda n: pl.BlockSpec((tm, n), lambda i: (i, 0))
    tab = pl.BlockSpec((tm, LANES), lambda i: (i % n_pos_blocks, 0))
    in_specs = [row(d), _const_spec((d, N_COLS)), tab, tab, tab, tab, _const_spec((1, LANES))]
    args = (x2d, w_r, c64, s64, c32, s32, bf128)
    if n_seq:
        col = lambda n: pl.BlockSpec((n, tm), lambda i: (0, i))
        seq = lambda n: pl.BlockSpec((1, n, tm), lambda i: (i // n_pos_blocks, 0, i % n_pos_blocks))
        rows_bf = lambda n: (sds((m, n), BF16), row(n))
        cols_bf = lambda n: (sds((n, m), BF16), col(n))
        seq_f32 = lambda n: (sds((n_seq, n, seq_len), F32), seq(n))
        outs = [rows_bf(D_HEADS), rows_bf(D_HEADS), rows_bf(W_IDX), (sds((m, LANES), F32), row(LANES)),
                cols_bf(D_HEADS), cols_bf(D_HEADS), cols_bf(W_IDX), cols_bf(D_VT), cols_bf(D_VT),
                seq_f32(D_HEADS), seq_f32(D_HEADS), seq_f32(D_HEADS), seq_f32(D_HEADS),
                seq_f32(IDX_DIM), seq_f32(N_HEADS), seq_f32(N_IDX_HEADS)]
        body = _proj_prompt_kernel
    else:
        rows = lambda n, dt: (sds((m, n), dt), row(n))
        heads = (sds((m * N_HEADS, HEAD_DIM), F32), pl.BlockSpec((tm * N_HEADS, HEAD_DIM), lambda i: (i, 0)))
        outs = ([rows(D_HEADS, BF16)] * 6 + [rows(W_IDX, BF16)] * 2 + [rows(LANES, F32)]
                + [heads] * 4 + [rows(IDX_DIM, F32), rows(N_HEADS, F32)])
        body = _proj_sample_kernel
    return pl.pallas_call(
        body,
        grid=(m // tm,),
        in_specs=in_specs,
        out_specs=[o[1] for o in outs],
        out_shape=[o[0] for o in outs],
        compiler_params=_params(("parallel",)),
        name="project",
    )(*args)


def _cumsum_kernel(x_ref, cumt_o, aux_o, carry_ref):
    t = x_ref.shape[1]

    @pl.when(pl.program_id(1) == 0)
    def _():
        carry_ref[...] = jnp.zeros_like(carry_ref)

    c = _dot3(_tri(t, upper=False), x_ref[0], a_is_f32=False) + carry_ref[...]
    carry_ref[...] = c[t - 1:t, :]
    cumt_o[0] = c.T[0:N_HEADS]
    n_hi, n_mid, n_lo = (piece.astype(F32) for piece in _split3(-c * LOG2E))
    lane = lax.broadcasted_iota(I32, (t, LANES), 1)
    aux = jnp.where(lane < N_HEADS, n_hi,
                    jnp.where(lane < 2 * N_HEADS, n_mid,
                              jnp.where(lane < L_ONE, n_lo,
                                        jnp.where(lane < L_ONE + N_SPLIT, 1.0, 0.0))))
    aux_o[0] = aux.astype(BF16)


def _cumsum(x, t):
    b, l, w = x.shape
    return pl.pallas_call(
        _cumsum_kernel,
        grid=(b, l // t),
        in_specs=[pl.BlockSpec((1, t, w), lambda i, j: (i, j, 0))],
        out_specs=[pl.BlockSpec((1, N_HEADS, t), lambda i, j: (i, 0, j)),
                   pl.BlockSpec((1, t, w), lambda i, j: (i, j, 0))],
        out_shape=[jax.ShapeDtypeStruct((b, N_HEADS, l), F32), jax.ShapeDtypeStruct((b, l, w), BF16)],
        scratch_shapes=[pltpu.VMEM((1, w), F32)],
        compiler_params=_params(("parallel", "arbitrary")),
        name="cumsum",
    )(x)


def _sortable(x):
    b = lax.bitcast_convert_type(x + 0.0, I32)
    return b ^ ((b >> 31) & 0x7FFFFFFF)


def _kth_threshold(count_ge, n_total, shape, k):
    zero = jnp.zeros(shape, I32)
    c0 = count_ge(zero)
    ok0 = c0 >= k
    t0 = jnp.where(ok0, zero, jnp.full(shape, INT_MIN, I32))
    n0 = jnp.where(ok0, c0, n_total)

    def cond(st):
        b, _, n = st
        return jnp.logical_and(b < 31, jnp.max(jnp.where(n != k, 1, 0)) > 0)

    def body(st):
        b, t, n = st
        for _ in range(BITS_PER_CHECK):
            cand = t | jnp.where(b < 31, jnp.left_shift(jnp.int32(1), jnp.maximum(30 - b, 0)), 0)
            c = count_ge(cand)
            ok = c >= k
            b, t, n = b + 1, jnp.where(ok, cand, t), jnp.where(ok, c, n)
        return b, t, n

    _, t, n = lax.while_loop(cond, body, (jnp.int32(0), t0, n0))
    return t, n


def _overflowing(n_ge, thr, k):
    neg_key = _sortable(jnp.full((1, 1), NEG, F32))
    return jnp.where(n_ge > k, jnp.where(thr > neg_key, 1, 0), 0)


def _init_state_t(tq):
    return jnp.full((N_HEADS, tq), NEG, F32), jnp.zeros((D_VT, tq), F32)


def _online_update_t(state, logits_and_values):
    m_all, acc_all = state
    pre = [logits_and_values(h) for h in range(N_HEADS)]
    m_out, acc_out = [], []
    for h, (s, vt) in enumerate(pre):
        m_prev = m_all[h:h + 1, :]
        m_new = jnp.maximum(m_prev, jnp.max(s, axis=0, keepdims=True))
        p = jnp.exp2(s - m_new).astype(BF16)
        m_out.append(m_new)
        acc_out.append(jnp.exp2(m_prev - m_new) * acc_all[VT_ROWS * h:VT_ROWS * (h + 1)]
                       + jnp.dot(vt, p, preferred_element_type=F32))
    return jnp.concatenate(m_out, axis=0), jnp.concatenate(acc_out, axis=0)


def _write_heads_t(o_ref, state):
    _, acc_all = state
    out_t = jnp.concatenate(
        [acc_all[VT_ROWS * h:VT_ROWS * h + HEAD_DIM] / acc_all[VT_ROWS * h + HEAD_DIM:VT_ROWS * h + HEAD_DIM + 1]
         for h in range(N_HEADS)], axis=0)
    o_ref[0] = out_t.T.astype(o_ref.dtype)


def _padded_head_queries_t(qt):
    zeros = jnp.zeros((HEAD_DIM, qt.shape[1]), qt.dtype)
    out = []
    for h in range(N_HEADS):
        qh = qt[HEAD_DIM * h:HEAD_DIM * (h + 1)]
        out.append(jnp.concatenate([qh, zeros] if h % 2 == 0 else [zeros, qh], axis=0))
    return out


def _fox_kernel(qt_ref, k_ref, aux_ref, vt_ref, cumt_ref, o_ref, *, tk):
    tq = qt_ref.shape[1]
    i = pl.program_id(1)
    cum_q = cumt_ref[0]
    row = lax.broadcasted_iota(I32, (LANES, tq), 0)
    rhs = []
    for h, qpad in enumerate(_padded_head_queries_t(qt_ref[...])):
        c_hi, c_mid, c_lo = (piece.astype(F32) for piece in _split3(cum_q[h:h + 1, :] * LOG2E))
        pick = jnp.where(row < L_ONE, jnp.where(row % N_HEADS == h, 1.0, 0.0), 0.0)
        sel = jnp.where(row == L_ONE, c_hi,
                        jnp.where(row == L_ONE + 1, c_mid, jnp.where(row == L_ONE + 2, c_lo, pick)))
        rhs.append(jnp.concatenate([qpad, sel.astype(BF16)], axis=0))
    q_pos = i * tq + lax.broadcasted_iota(I32, (1, tq), 1)

    def block(j, state, masked):
        c0 = pl.multiple_of(j * tk, tk)
        aux = aux_ref[0, pl.ds(c0, tk), :]
        if masked:
            keep = (c0 + lax.broadcasted_iota(I32, (tk, 1), 0)) <= q_pos

        def logits_and_values(h):
            pair = h // 2
            lhs = jnp.concatenate([k_ref[0, pl.ds(c0, tk), LANES * pair:LANES * (pair + 1)], aux], axis=1)
            s = jnp.dot(lhs, rhs[h], preferred_element_type=F32)
            if masked:
                s = jnp.where(keep, s, NEG)
            return s, vt_ref[VT_ROWS * h:VT_ROWS * (h + 1), pl.ds(c0, tk)]

        return _online_update_t(state, logits_and_values)

    n_full = (i * tq) // tk
    state = lax.fori_loop(0, n_full, lambda j, st: block(j, st, False), _init_state_t(tq))
    state = block(n_full, state, True)
    _write_heads_t(o_ref, state)


def _fox_prompt(qt, k, aux, vt, cumt, tq, tk):
    b, l, d = k.shape
    nq = l // tq
    return pl.pallas_call(
        functools.partial(_fox_kernel, tk=tk),
        grid=(b, nq),
        in_specs=[pl.BlockSpec((d, tq), lambda bi, i: (0, bi * nq + i)),
                  pl.BlockSpec((1, l, d), lambda bi, i: (bi, 0, 0)),
                  pl.BlockSpec((1, l, LANES), lambda bi, i: (bi, 0, 0)),
                  pl.BlockSpec((D_VT, l), lambda bi, i: (0, bi)),
                  pl.BlockSpec((1, N_HEADS, tq), lambda bi, i: (bi, 0, i))],
        out_specs=pl.BlockSpec((1, tq, d), lambda bi, i: (bi, i, 0)),
        out_shape=jax.ShapeDtypeStruct((b, l, d), BF16),
        compiler_params=_params(("parallel", "arbitrary")),
        name="fox_prompt",
    )(qt, k, aux, vt, cumt)


def _dsa_kernel(iqt_ref, ikr_ref, iwt_ref, qt_ref, k_ref, vt_ref, o_ref, key_ref, *, tk, topk):
    tq = qt_ref.shape[1]
    i = pl.program_id(1)
    n_blk = (i * tq) // tk + 1
    q_chunk = (i * tq + lax.broadcasted_iota(I32, (1, tq), 1)) // CHUNK

    def admissible(c0):
        return (c0 + lax.broadcasted_iota(I32, (tk, 1), 0)) // CHUNK <= q_chunk

    iqt = iqt_ref[...]
    iq_rhs = []
    for h in range(N_IDX_HEADS):
        parts = [jnp.zeros((IDX_DIM * h, tq), iqt.dtype), iqt[IDX_DIM * h:IDX_DIM * (h + 1)],
                 jnp.zeros((IDX_DIM * (N_IDX_HEADS - 1 - h), tq), iqt.dtype)]
        iq_rhs.append(jnp.concatenate([p for p in parts if p.shape[0]], axis=0))
    iw = iwt_ref[0]

    def score_block(j, carry):
        c0 = pl.multiple_of(j * tk, tk)
        kb = ikr_ref[0, pl.ds(c0, tk), :]
        logits = [jnp.dot(kb, iq_rhs[h], preferred_element_type=F32) for h in range(N_IDX_HEADS)]
        score = iw[0:1, :] * jnp.maximum(logits[0], 0.0)
        for h in range(1, N_IDX_HEADS):
            score = score + iw[h:h + 1, :] * jnp.maximum(logits[h], 0.0)
        key_ref[pl.ds(c0, tk), :] = _sortable(jnp.where(admissible(c0), score, NEG))
        return carry

    lax.fori_loop(0, n_blk, score_block, 0)

    def count(pred):
        def body(j, acc):
            c0 = pl.multiple_of(j * tk, tk)
            hit = jnp.where(pred(key_ref[pl.ds(c0, tk), :]), 1, 0).astype(I32)
            return acc + jnp.sum(hit.reshape(tk // SUBLANES, SUBLANES, tq), axis=0)
        part = lax.fori_loop(0, n_blk, body, jnp.zeros((SUBLANES, tq), I32))
        return jnp.sum(part, axis=0, keepdims=True)

    thr, n_ge = _kth_threshold(lambda t: count(lambda kb: kb >= t), n_blk * tk, (1, tq), topk)

    @pl.when(jnp.max(_overflowing(n_ge, thr, topk)) > 0)
    def _():
        take = (topk - count(lambda kb: kb > thr)).astype(F32)
        strict_lower = (lax.broadcasted_iota(I32, (tk, tk), 0)
                        > lax.broadcasted_iota(I32, (tk, tk), 1)).astype(BF16)

        def body(j, seen):
            c0 = pl.multiple_of(j * tk, tk)
            kb = key_ref[pl.ds(c0, tk), :]
            tie = kb == thr
            tie_f = jnp.where(tie, 1.0, 0.0)
            rank = jnp.dot(strict_lower, tie_f.astype(BF16), preferred_element_type=F32) + seen
            drop = jnp.where(tie, jnp.where(rank >= take, 1, 0), 0)
            key_ref[pl.ds(c0, tk), :] = jnp.where(drop > 0, INT_MIN, kb)
            return seen + jnp.sum(tie_f, axis=0, keepdims=True)

        lax.fori_loop(0, n_blk, body, jnp.zeros((1, tq), F32))

    rhs = _padded_head_queries_t(qt_ref[...])

    def attend_block(j, state):
        c0 = pl.multiple_of(j * tk, tk)
        bias = jnp.where(key_ref[pl.ds(c0, tk), :] >= thr, jnp.where(admissible(c0), 0.0, NEG), NEG)

        def logits_and_values(h):
            pair = h // 2
            kb = k_ref[0, pl.ds(c0, tk), LANES * pair:LANES * (pair + 1)]
            s = jnp.dot(kb, rhs[h], preferred_element_type=F32) + bias
            return s, vt_ref[VT_ROWS * h:VT_ROWS * (h + 1), pl.ds(c0, tk)]

        return _online_update_t(state, logits_and_values)

    _write_heads_t(o_ref, lax.fori_loop(0, n_blk, attend_block, _init_state_t(tq)))


def _dsa_prompt(iqt, ikr, iwt, qt, k, vt, tq, tk, topk):
    b, l, d = k.shape
    nq = l // tq
    w_idx = ikr.shape[2]
    qcol = lambda n: pl.BlockSpec((n, tq), lambda bi, i: (0, bi * nq + i))
    full = lambda n: pl.BlockSpec((1, l, n), lambda bi, i: (bi, 0, 0))
    return pl.pallas_call(
        functools.partial(_dsa_kernel, tk=tk, topk=topk),
        grid=(b, nq),
        in_specs=[qcol(w_idx), full(w_idx), pl.BlockSpec((1, N_IDX_HEADS, tq), lambda bi, i: (bi, 0, i)),
                  qcol(d), full(d), pl.BlockSpec((D_VT, l), lambda bi, i: (0, bi))],
        out_specs=pl.BlockSpec((1, tq, d), lambda bi, i: (bi, i, 0)),
        out_shape=jax.ShapeDtypeStruct((b, l, d), BF16),
        scratch_shapes=[pltpu.VMEM((l, tq), I32)],
        compiler_params=_params(("parallel", "arbitrary")),
        name="dsa_prompt",
    )(iqt, ikr, iwt, qt, k, vt)


def _pair_queries(q, pair):
    t = q.shape[0]
    low = lax.broadcasted_iota(I32, (t, LANES), 1) < HEAD_DIM
    zero = jnp.zeros((t, LANES), q.dtype)
    slab = q[:, LANES * pair:LANES * (pair + 1)]
    return jnp.concatenate([jnp.where(low, slab, zero), jnp.where(low, zero, slab)], axis=0)


def _nt_dot(a, b):
    return lax.dot_general(a, b, (((1,), (1,)), ((), ())), preferred_element_type=F32)


def _lane_fold_count(mask):
    ones = jnp.where(mask, 1, 0).astype(I32)
    acc = ones[:, 0:LANES]
    for c in range(1, mask.shape[1] // LANES):
        acc = acc + ones[:, LANES * c:LANES * (c + 1)]
    return acc


def _pair_attention(q_pair, kt_past, vt_past, k_new, v_new, bias_past, bias_new):
    t = k_new.shape[0]
    s_past = jnp.dot(q_pair, kt_past, preferred_element_type=F32) + bias_past
    s_new = _nt_dot(q_pair, k_new) + bias_new
    m = jnp.maximum(jnp.max(s_past, axis=1, keepdims=True), jnp.max(s_new, axis=1, keepdims=True))
    p_past = jnp.exp(s_past - m)
    p_new = jnp.exp(s_new - m)
    denom = jnp.sum(p_past, axis=1, keepdims=True) + jnp.sum(p_new, axis=1, keepdims=True)
    o = (_nt_dot(p_past.astype(BF16), vt_past)
         + jnp.dot(p_new.astype(BF16), v_new, preferred_element_type=F32)) / denom
    low = lax.broadcasted_iota(I32, (t, LANES), 1) < HEAD_DIM
    return jnp.where(low, o[0:t], o[t:2 * t])


def _sample_kernel(fq_ref, fkn_ref, fvn_ref, dq_ref, dkn_ref, dvn_ref, iq_ref, ikrn_ref, misc_ref,
                   cfk_ref, cfv_ref, cdk_ref, cdv_ref, cik_ref, clogf_ref,
                   o_ref, kp_ref, kn_ref, *, topk):
    t = fq_ref.shape[1]
    p_len = cfk_ref.shape[2]
    n_pb = p_len // LANES
    row = lax.broadcasted_iota(I32, (t, 1), 0)
    col_new = lax.broadcasted_iota(I32, (1, t), 1)
    stack2 = lambda a: jnp.concatenate([a, a], axis=0)

    n_rows = N_HEADS * n_pb
    within = _dot3(clogf_ref[0], _tri(LANES, upper=True), a_is_f32=True)
    block_tot = jnp.broadcast_to(within[:, LANES - 1:LANES], (n_rows, LANES))
    r = lax.broadcasted_iota(I32, (n_rows, n_rows), 0)
    c = lax.broadcasted_iota(I32, (n_rows, n_rows), 1)
    same_head = (r // n_pb) == (c // n_pb)
    earlier = jnp.where(same_head, jnp.where(c < r, 1.0, 0.0), 0.0).astype(BF16)
    cum_past = within + _dot3(earlier, block_tot, a_is_f32=False)
    head_tot = _dot3(jnp.where(same_head, 1.0, 0.0).astype(BF16), block_tot, a_is_f32=False)
    misc = misc_ref[0]
    pre_new = _dot3(_tri(t, upper=False), misc, a_is_f32=False)
    pre_new_t = _dot3(misc.T, _tri(t, upper=True), a_is_f32=True)
    cq, ck_past, ck_new = [], [], []
    for h in range(N_HEADS):
        tot = head_tot[h * n_pb:h * n_pb + 1, :]
        cq.append(pre_new[:, h:h + 1] + tot[:, 0:1])
        ck_past.append(jnp.concatenate([cum_past[h * n_pb + b:h * n_pb + b + 1, :] for b in range(n_pb)], axis=1))
        ck_new.append(pre_new_t[h:h + 1, :] + tot[:, 0:t])

    causal_new = col_new <= row
    fq = fq_ref[0]
    for pair in range(N_PAIRS):
        sl = slice(LANES * pair, LANES * (pair + 1))
        h0, h1 = 2 * pair, 2 * pair + 1
        bias_past = jnp.concatenate([cq[h0] - ck_past[h0], cq[h1] - ck_past[h1]], axis=0)
        bias_new = jnp.concatenate([jnp.where(causal_new, cq[h0] - ck_new[h0], NEG),
                                    jnp.where(causal_new, cq[h1] - ck_new[h1], NEG)], axis=0)
        o_ref[0, :, sl] = _pair_attention(
            _pair_queries(fq, pair), cfk_ref[0, sl, :].astype(BF16), cfv_ref[0, sl, :].astype(BF16),
            fkn_ref[0, :, sl], fvn_ref[0, :, sl], bias_past, bias_new).astype(o_ref.dtype)

    iq = iq_ref[0]
    lane_head = lax.broadcasted_iota(I32, iq.shape, 1) // IDX_DIM
    iq_stack = jnp.concatenate([jnp.where(lane_head == h, iq, jnp.zeros_like(iq))
                                for h in range(N_IDX_HEADS)], axis=0)
    ikt = cik_ref[0].astype(BF16)
    lg_past = jnp.dot(iq_stack, jnp.concatenate([ikt] * N_IDX_HEADS, axis=0), preferred_element_type=F32)
    lg_new = _nt_dot(iq_stack, ikrn_ref[0])
    sc_past = jnp.zeros((t, p_len), F32)
    sc_new = jnp.zeros((t, t), F32)
    for h in range(N_IDX_HEADS):
        w = misc[:, L_IW + h:L_IW + h + 1]
        sc_past = sc_past + w * jnp.maximum(lg_past[h * t:(h + 1) * t], 0.0)
        sc_new = sc_new + w * jnp.maximum(lg_new[h * t:(h + 1) * t], 0.0)
    row_chunk = (p_len + row) // CHUNK
    adm_past = (lax.broadcasted_iota(I32, (1, p_len), 1) // CHUNK) <= row_chunk
    adm_new = ((p_len + col_new) // CHUNK) <= row_chunk
    kp_ref[...] = _sortable(jnp.where(adm_past, sc_past, NEG))
    kn_ref[...] = _sortable(jnp.where(adm_new, sc_new, NEG))

    def count(pred):
        c_past = jnp.sum(_lane_fold_count(pred(kp_ref[...])), axis=1, keepdims=True)
        return c_past + jnp.sum(jnp.where(pred(kn_ref[...]), 1, 0).astype(I32), axis=1, keepdims=True)

    thr, n_ge = _kth_threshold(lambda x: count(lambda kb: kb >= x), p_len + t, (t, 1), topk)

    @pl.when(jnp.max(_overflowing(n_ge, thr, topk)) > 0)
    def _():
        take = (topk - count(lambda kb: kb > thr)).astype(F32)
        seen = jnp.zeros((t, 1), F32)
        for ref, n in ((kp_ref, p_len), (kn_ref, t)):
            strict_upper = (lax.broadcasted_iota(I32, (n, n), 0)
                            < lax.broadcasted_iota(I32, (n, n), 1)).astype(BF16)
            kb = ref[...]
            tie = kb == thr
            tie_f = jnp.where(tie, 1.0, 0.0)
            rank = jnp.dot(tie_f.astype(BF16), strict_upper, preferred_element_type=F32) + seen
            drop = jnp.where(tie, jnp.where(rank >= take, 1, 0), 0)
            ref[...] = jnp.where(drop > 0, INT_MIN, kb)
            seen = seen + jnp.sum(tie_f, axis=1, keepdims=True)

    bias_past = stack2(jnp.where(kp_ref[...] >= thr, jnp.where(adm_past, 0.0, NEG), NEG))
    bias_new = stack2(jnp.where(kn_ref[...] >= thr, jnp.where(adm_new, 0.0, NEG), NEG))
    dq = dq_ref[0]
    for pair in range(N_PAIRS):
        sl = slice(LANES * pair, LANES * (pair + 1))
        o_ref[0, :, D_HEADS + LANES * pair:D_HEADS + LANES * (pair + 1)] = _pair_attention(
            _pair_queries(dq, pair), cdk_ref[0, sl, :].astype(BF16), cdv_ref[0, sl, :].astype(BF16),
            dkn_ref[0, :, sl], dvn_ref[0, :, sl], bias_past, bias_new).astype(o_ref.dtype)


def _sample_mixers(fq, fkn, fvn, dq, dkn, dvn, iq, ikrn, misc, cfk, cfv, cdk, cdv, cik, clogf, topk):
    b, t, d = fq.shape
    p_len = cfk.shape[2]
    new = lambda n: pl.BlockSpec((1, t, n), lambda bi: (bi, 0, 0))
    whole = lambda a: pl.BlockSpec((1,) + a.shape[1:], lambda bi: (bi, 0, 0))
    return pl.pallas_call(
        functools.partial(_sample_kernel, topk=topk),
        grid=(b,),
        in_specs=[new(d), new(d), new(d), new(d), new(d), new(d), new(iq.shape[2]), new(ikrn.shape[2]), new(LANES),
                  whole(cfk), whole(cfv), whole(cdk), whole(cdv), whole(cik), whole(clogf)],
        out_specs=pl.BlockSpec((1, t, 2 * d), lambda bi: (bi, 0, 0)),
        out_shape=jax.ShapeDtypeStruct((b, t, 2 * d), BF16),
        scratch_shapes=[pltpu.VMEM((t, p_len), I32), pltpu.VMEM((t, t), I32)],
        compiler_params=_params(("parallel",)),
        name="sample_mixers",
    )(fq, fkn, fvn, dq, dkn, dvn, iq, ikrn, misc, cfk, cfv, cdk, cdv, cik, clogf)


def _layer_norm(x, g, b):
    mu = jnp.mean(x, axis=-1, keepdims=True)
    xc = x - mu
    var = jnp.mean(xc * xc, axis=-1, keepdims=True)
    return xc * lax.rsqrt(var + LN_EPS) * g + b


def _finish_kernel(x_ref, mixf_ref, mixd_ref, p_ref, wo_ref, g1_ref, b1_ref, wup_ref, wdn_ref,
                   g2_ref, b2_ref, wple_ref, wg_ref, bg_ref, o_ref, *, alpha, ff_chunk):
    d_mix_half = mixf_ref.shape[1]
    a = (alpha * x_ref[...]
         + jnp.dot(mixf_ref[...], wo_ref[0:d_mix_half, :], preferred_element_type=F32)
         + jnp.dot(mixd_ref[...], wo_ref[d_mix_half:, :], preferred_element_type=F32))
    x1 = _layer_norm(a, g1_ref[...], b1_ref[...])
    x1b = x1.astype(BF16)
    ffn = jnp.zeros_like(x1)
    for c in range(wup_ref.shape[1] // ff_chunk):
        sl = slice(ff_chunk * c, ff_chunk * (c + 1))
        hid = jnp.maximum(jnp.dot(x1b, wup_ref[:, sl], preferred_element_type=F32), 0.0)
        ffn = ffn + jnp.dot((hid * hid).astype(BF16), wdn_ref[sl, :], preferred_element_type=F32)
    x2 = _layer_norm(alpha * x1 + ffn, g2_ref[...], b2_ref[...])
    gate = jax.nn.sigmoid(jnp.dot(x2.astype(BF16), wg_ref[...], preferred_element_type=F32) + bg_ref[...])
    pe = jnp.dot(p_ref[...].astype(BF16), wple_ref[...], preferred_element_type=F32)
    o_ref[...] = x2 + gate * pe


def _finish(x2d, mixes, p2d, w, alpha, tm):
    m, d = x2d.shape
    tm = min(tm, m)
    row = lambda n: pl.BlockSpec((tm, n), lambda i: (i, 0))
    if len(mixes) == 1:
        half = mixes[0].shape[1] // 2
        mix_specs = [pl.BlockSpec((tm, half), lambda i: (i, 0)), pl.BlockSpec((tm, half), lambda i: (i, 1))]
        mix_args = [mixes[0], mixes[0]]
    else:
        mix_specs = [row(mixes[0].shape[1]), row(mixes[1].shape[1])]
        mix_args = list(mixes)
    d_ff = w["w_up"].shape[1]
    vec = lambda n: _const_spec((1, n))
    return pl.pallas_call(
        functools.partial(_finish_kernel, alpha=alpha, ff_chunk=min(d_ff, 1024)),
        grid=(m // tm,),
        in_specs=[row(d)] + mix_specs + [row(p2d.shape[1]),
                  _const_spec(w["w_o"].shape), vec(d), vec(d),
                  _const_spec(w["w_up"].shape), _const_spec(w["w_down"].shape), vec(d), vec(d),
                  _const_spec(w["w_ple"].shape), _const_spec(w["w_ple_gate"].shape), vec(d)],
        out_specs=row(d),
        out_shape=jax.ShapeDtypeStruct((m, d), F32),
        compiler_params=_params(("parallel",)),
        name="finish",
    )(x2d, *mix_args, p2d, w["w_o"], w["ln1_g"], w["ln1_b"], w["w_up"], w["w_down"],
      w["ln2_g"], w["ln2_b"], w["w_ple"], w["w_ple_gate"], w["b_ple_gate"])


def _relayout_w_in(w_in):
    widths = (("fq", D_HEADS), ("fk", D_HEADS), ("fv", D_HEADS), ("fg", N_HEADS),
              ("dq", D_HEADS), ("dk", D_HEADS), ("dv", D_HEADS),
              ("iq", W_IDX), ("ik", IDX_DIM), ("iw", N_IDX_HEADS))
    cols, o = {}, 0
    for name, n in widths:
        cols[name] = w_in[:, o:o + n]
        o += n
    pad = jnp.zeros((w_in.shape[0], LANES - N_SPLIT * N_HEADS - N_IDX_HEADS), w_in.dtype)
    w_r = jnp.concatenate([cols["fq"], cols["fk"], cols["fv"], cols["dq"], cols["dk"], cols["dv"],
                           cols["iq"], jnp.tile(cols["ik"], (1, N_IDX_HEADS)),
                           jnp.tile(cols["fg"], (1, N_SPLIT)), cols["iw"], pad], axis=1)
    return w_r.astype(BF16)


def _layer(x_p, x_s, p_p, p_s, c_fk, c_fv, c_logf, c_dk, c_dv, c_ik, w, alpha):
    b, l, d = x_p.shape
    bs, t, _ = x_s.shape
    p_len = c_fk.shape[1]
    w_r = _relayout_w_in(w["w_in"])

    (fk, dk, ikr, logf3, fqt, dqt, iqt, fvt, dvt,
     fkt32, fvt32, dkt32, dvt32, ikt32, logft, iwt) = _project(
        x_p.reshape(b * l, d), w_r, w["b_f"], jnp.arange(l, dtype=I32), tm=512, n_seq=b)
    r3 = lambda a: a.reshape((b, l) + a.shape[1:])
    cumt, aux = _cumsum(r3(logf3), 512)
    mix_fox = _fox_prompt(fqt, r3(fk), aux, fvt, cumt, tq=256, tk=512)
    mix_dsa = _dsa_prompt(iqt, r3(ikr), iwt, dqt, r3(dk), dvt, tq=256, tk=512, topk=min(TOPK_MAX, l // 4))
    y_p = _finish(x_p.reshape(b * l, d), (mix_fox.reshape(b * l, -1), mix_dsa.reshape(b * l, -1)),
                  p_p.reshape(b * l, -1), w, alpha, tm=512).reshape(b, l, d)
    heads_first = lambda a: jnp.transpose(a.reshape(b, N_HEADS, HEAD_DIM, l), (0, 3, 1, 2))
    pos_last = lambda a: jnp.swapaxes(a, 1, 2)
    rows_p = (heads_first(fkt32), heads_first(fvt32), pos_last(logft),
              heads_first(dkt32), heads_first(dvt32), pos_last(ikt32))

    (sfq, sfk, sfv, sdq, sdk, sdv, siq, sikr, smisc, sfk32, sfv32, sdk32, sdv32, sik32, slogf) = _project(
        x_s.reshape(bs * t, d), w_r, w["b_f"], p_len + jnp.arange(t, dtype=I32), tm=512, n_seq=0)
    s3 = lambda a: a.reshape((bs, t) + a.shape[1:])
    cache_t = lambda c: jnp.transpose(c, (0, 2, 3, 1)).reshape(bs, D_HEADS, p_len)
    clogf = jnp.swapaxes(c_logf.astype(F32), 1, 2).reshape(bs, N_HEADS * (p_len // LANES), LANES)
    mix_s = _sample_mixers(s3(sfq), s3(sfk), s3(sfv), s3(sdq), s3(sdk), s3(sdv), s3(siq), s3(sikr), s3(smisc),
                           cache_t(c_fk), cache_t(c_fv), cache_t(c_dk), cache_t(c_dv),
                           jnp.swapaxes(c_ik, 1, 2), clogf, topk=min(TOPK_MAX, (p_len + t) // 4))
    y_s = _finish(x_s.reshape(bs * t, d), (mix_s.reshape(bs * t, -1),), p_s.reshape(bs * t, -1),
                  w, alpha, tm=512).reshape(bs, t, d)
    heads_s = lambda a: a.reshape(bs, t, N_HEADS, HEAD_DIM)
    rows_s = (heads_s(sfk32), heads_s(sfv32), s3(slogf), heads_s(sdk32), heads_s(sdv32), s3(sik32))
    return y_p, y_s, rows_p, rows_s


def kernel(x_prompt, x_sample, p_prompt, p_sample, cache_fox_k, cache_fox_v, cache_fox_logf, cache_dsa_k, cache_dsa_v, cache_idx_k, w_in, b_f, w_o, ln1_g, ln1_b, w_up, w_down, ln2_g, ln2_b, w_ple, w_ple_gate, b_ple_gate):
    depth = w_in.shape[0]
    alpha = (2 * depth) ** 0.25
    y_p, y_s = x_prompt, x_sample
    new_rows = [[] for _ in range(12)]
    for i in range(depth):
        vec = lambda a: a[i].reshape(1, -1)
        w = {"w_in": w_in[i], "b_f": b_f[i], "w_o": w_o[i].astype(BF16),
             "ln1_g": vec(ln1_g), "ln1_b": vec(ln1_b),
             "w_up": w_up[i].astype(BF16), "w_down": w_down[i].astype(BF16),
             "ln2_g": vec(ln2_g), "ln2_b": vec(ln2_b),
             "w_ple": w_ple[i].astype(BF16), "w_ple_gate": w_ple_gate[i].astype(BF16),
             "b_ple_gate": vec(b_ple_gate)}
        y_p, y_s, rows_p, rows_s = _layer(
            y_p, y_s, p_prompt[i], p_sample[i], cache_fox_k[i], cache_fox_v[i], cache_fox_logf[i],
            cache_dsa_k[i], cache_dsa_v[i], cache_idx_k[i], w, alpha)
        for lst, a in zip(new_rows, rows_p + rows_s):
            lst.append(a)
    return (y_p, y_s) + tuple(jnp.stack(a, axis=0) for a in new_rows)
```

```python
import functools

import jax
import jax.numpy as jnp
from jax import lax
from jax.experimental import pallas as pl
from jax.experimental.pallas import tpu as pltpu

F32 = jnp.float32
BF16 = jnp.bfloat16
I32 = jnp.int32
I16 = jnp.int16

HEAD_DIM = 64
N_HEADS = 8
N_PAIRS = N_HEADS // 2
IDX_DIM = 32
N_IDX_HEADS = 8
CHUNK = 64
TOPK_MAX = 256
ROPE_THETA = 10000.0
LN_EPS = 1e-5
NEG = -1e30
LANES = 128
SUBLANES = 8
INT_MIN = -2 ** 31
BITS_PER_CHECK = 4
LOG2E = 1.4426950408889634
VT_ROWS = 80
D_VT = N_HEADS * VT_ROWS

D_HEADS = N_HEADS * HEAD_DIM
C_FQ, C_FK, C_FV, C_DQ, C_DK, C_DV = (i * D_HEADS for i in range(6))
W_IDX = N_IDX_HEADS * IDX_DIM
C_IQ = 6 * D_HEADS
C_IKR = C_IQ + W_IDX
C_MISC = C_IKR + W_IDX
N_COLS = C_MISC + LANES
N_SPLIT = 3
L_IW = N_SPLIT * N_HEADS
L_ONE = N_SPLIT * N_HEADS

VMEM_LIMIT = 56 * 1024 * 1024


def _params(sem):
    return pltpu.CompilerParams(dimension_semantics=sem, vmem_limit_bytes=VMEM_LIMIT)


def _const_spec(shape):
    nd = len(shape)
    return pl.BlockSpec(shape, lambda *_: (0,) * nd, pipeline_mode=pl.Buffered(1))


def _per_batch_spec(shape, index_map):
    return pl.BlockSpec(shape, index_map, pipeline_mode=pl.Buffered(1))


def _split3(x):
    hi = x.astype(BF16)
    r = x - hi.astype(F32)
    mid = r.astype(BF16)
    lo = (r - mid.astype(F32)).astype(BF16)
    return hi, mid, lo


def _dot3(a, b, a_is_f32):
    if a_is_f32:
        return sum(jnp.dot(p, b, preferred_element_type=F32) for p in _split3(a))
    return sum(jnp.dot(a, p, preferred_element_type=F32) for p in _split3(b))


def _tri(n, upper):
    r = lax.broadcasted_iota(I32, (n, n), 0)
    c = lax.broadcasted_iota(I32, (n, n), 1)
    return (r <= c if upper else r >= c).astype(BF16)


def _rope_slab(t, cs, sn, first, half):
    sw = jnp.where(first, pltpu.roll(t, LANES - half, 1), pltpu.roll(t, half, 1))
    return t * cs + sw * sn


def _rope(val, cs, sn, first, half):
    return jnp.concatenate([_rope_slab(val[:, LANES * j:LANES * (j + 1)], cs, sn, first, half)
                            for j in range(val.shape[1] // LANES)], axis=1)


def _with_ones_rows(vt, tm):
    ones = jnp.ones((VT_ROWS - HEAD_DIM, tm), F32)
    parts = []
    for h in range(N_HEADS):
        parts += [vt[HEAD_DIM * h:HEAD_DIM * (h + 1)], ones]
    return jnp.concatenate(parts, axis=0)


def _proj_core(x_ref, w_ref, c64_ref, s64_ref, c32_ref, s32_ref, bf_ref, q_scale):
    tm = x_ref.shape[0]
    xb = x_ref[...].astype(BF16)

    def mm(c0, n):
        return jnp.dot(xb, w_ref[:, c0:c0 + n], preferred_element_type=F32)

    lane = lax.broadcasted_iota(I32, (tm, LANES), 1)
    first64 = (lane % HEAD_DIM) < HEAD_DIM // 2
    first32 = (lane % IDX_DIM) < IDX_DIM // 2
    c64, s64, c32, s32 = c64_ref[...], s64_ref[...], c32_ref[...], s32_ref[...]
    misc = mm(C_MISC, LANES)
    z = misc + bf_ref[...]
    logf = jnp.minimum(z, 0.0) - jnp.log1p(jnp.exp(-jnp.abs(z)))
    misc = jnp.where(lane < L_IW, logf,
                     jnp.where(lane < L_IW + N_IDX_HEADS, misc * (N_IDX_HEADS ** -0.5 * IDX_DIM ** -0.5), 0.0))
    return dict(
        fq=mm(C_FQ, D_HEADS) * q_scale, fk=mm(C_FK, D_HEADS), fv=mm(C_FV, D_HEADS),
        dq=_rope(mm(C_DQ, D_HEADS), c64, s64, first64, HEAD_DIM // 2) * q_scale,
        dk=_rope(mm(C_DK, D_HEADS), c64, s64, first64, HEAD_DIM // 2), dv=mm(C_DV, D_HEADS),
        iq=_rope(mm(C_IQ, W_IDX), c32, s32, first32, IDX_DIM // 2),
        ikr=_rope(mm(C_IKR, W_IDX), c32, s32, first32, IDX_DIM // 2),
        misc=misc, lane=lane)


def _proj_prompt_kernel(x_ref, w_ref, c64_ref, s64_ref, c32_ref, s32_ref, bf_ref,
                        fk_o, dk_o, ikr_o, logf3_o, fqt_o, dqt_o, iqt_o, fvt_o, dvt_o,
                        fkt32_o, fvt32_o, dkt32_o, dvt32_o, ikt32_o, logft_o, iwt_o):
    tm = x_ref.shape[0]
    r = _proj_core(x_ref, w_ref, c64_ref, s64_ref, c32_ref, s32_ref, bf_ref, HEAD_DIM ** -0.5 * LOG2E)
    fk_o[...] = r["fk"].astype(BF16)
    dk_o[...] = r["dk"].astype(BF16)
    ikr_o[...] = r["ikr"].astype(BF16)
    logf3_o[...] = jnp.where(r["lane"] < L_IW, r["misc"], 0.0)
    fqt_o[...] = r["fq"].T.astype(BF16)
    dqt_o[...] = r["dq"].T.astype(BF16)
    iqt_o[...] = r["iq"].T.astype(BF16)
    for name, o32, obf in (("fv", fvt32_o, fvt_o), ("dv", dvt32_o, dvt_o)):
        vt = r[name].T
        o32[0] = vt
        obf[...] = _with_ones_rows(vt, tm).astype(BF16)
    fkt32_o[0] = r["fk"].T
    dkt32_o[0] = r["dk"].T
    ikt32_o[0] = r["ikr"][:, 0:LANES].T[0:IDX_DIM]
    misc_t = r["misc"].T
    logft_o[0] = misc_t[0:N_HEADS]
    iwt_o[0] = misc_t[L_IW:L_IW + N_IDX_HEADS]


def _store_heads(o_ref, val):
    tm = val.shape[0]
    for h in range(N_HEADS):
        o_ref[pl.ds(h, tm, stride=N_HEADS), :] = val[:, HEAD_DIM * h:HEAD_DIM * (h + 1)]


def _proj_sample_kernel(x_ref, w_ref, c64_ref, s64_ref, c32_ref, s32_ref, bf_ref,
                        fq_o, fk_o, fv_o, dq_o, dk_o, dv_o, iq_o, ikr_o, misc_o,
                        fk32_o, fv32_o, dk32_o, dv32_o, ik32_o, logf_o):
    r = _proj_core(x_ref, w_ref, c64_ref, s64_ref, c32_ref, s32_ref, bf_ref, HEAD_DIM ** -0.5)
    for name, o in (("fq", fq_o), ("fk", fk_o), ("fv", fv_o), ("dq", dq_o), ("dk", dk_o), ("dv", dv_o),
                    ("iq", iq_o), ("ikr", ikr_o)):
        o[...] = r[name].astype(BF16)
    misc_o[...] = r["misc"]
    for name, o in (("fk", fk32_o), ("fv", fv32_o), ("dk", dk32_o), ("dv", dv32_o)):
        _store_heads(o, r[name])
    ik32_o[...] = r["ikr"][:, 0:IDX_DIM]
    logf_o[...] = r["misc"][:, 0:N_HEADS]


def _rope_tables(pos, dim):
    half = dim // 2
    inv_freq = ROPE_THETA ** (-jnp.arange(half, dtype=F32) / half)
    ang = pos.astype(F32)[:, None] * inv_freq[None, :]
    cos = jnp.cos(ang)
    sin = jnp.sin(ang)
    reps = LANES // dim
    cos_t = jnp.tile(jnp.concatenate([cos, cos], axis=1), (1, reps))
    sin_t = jnp.tile(jnp.concatenate([-sin, sin], axis=1), (1, reps))
    return cos_t, sin_t


def _project(x2d, w_r, b_f, pos, tm, n_seq):
    m, d = x2d.shape
    tm = min(tm, m)
    n_pos = pos.shape[0]
    seq_len = n_pos
    if n_pos < tm:
        pos = jnp.tile(pos, tm // n_pos)
        n_pos = tm
    n_pos_blocks = n_pos // tm
    c64, s64 = _rope_tables(pos, HEAD_DIM)
    c32, s32 = _rope_tables(pos, IDX_DIM)
    bf128 = jnp.concatenate([jnp.tile(b_f.reshape(1, N_HEADS), (1, N_SPLIT)),
                             jnp.zeros((1, LANES - N_SPLIT * N_HEADS), F32)], axis=1)

    sds = jax.ShapeDtypeStruct
    row = lambda n: pl.BlockSpec((tm, n), lambda i: (i, 0))
    tab = pl.BlockSpec((tm, LANES), lambda i: (i % n_pos_blocks, 0))
    in_specs = [row(d), _const_spec((d, N_COLS)), tab, tab, tab, tab, _const_spec((1, LANES))]
    args = (x2d, w_r, c64, s64, c32, s32, bf128)
    if n_seq:
        col = lambda n: pl.BlockSpec((n, tm), lambda i: (0, i))
        seq = lambda n: pl.BlockSpec((1, n, tm), lambda i: (i // n_pos_blocks, 0, i % n_pos_blocks))
        rows_bf = lambda n: (sds((m, n), BF16), row(n))
        cols_bf = lambda n: (sds((n, m), BF16), col(n))
        seq_f32 = lambda n: (sds((n_seq, n, seq_len), F32), seq(n))
        outs = [rows_bf(D_HEADS), rows_bf(D_HEADS), rows_bf(W_IDX), (sds((m, LANES), F32), row(LANES)),
                cols_bf(D_HEADS), cols_bf(D_HEADS), cols_bf(W_IDX), cols_bf(D_VT), cols_bf(D_VT),
                seq_f32(D_HEADS), seq_f32(D_HEADS), seq_f32(D_HEADS), seq_f32(D_HEADS),
                seq_f32(IDX_DIM), seq_f32(N_HEADS), seq_f32(N_IDX_HEADS)]
        body = _proj_prompt_kernel
    else:
        rows = lambda n, dt: (sds((m, n), dt), row(n))
        heads = (sds((m * N_HEADS, HEAD_DIM), F32), pl.BlockSpec((tm * N_HEADS, HEAD_DIM), lambda i: (i, 0)))
        outs = ([rows(D_HEADS, BF16)] * 6 + [rows(W_IDX, BF16)] * 2 + [rows(LANES, F32)]
                + [heads] * 4 + [rows(IDX_DIM, F32), rows(N_HEADS, F32)])
        body = _proj_sample_kernel
    return pl.pallas_call(
        body,
        grid=(m // tm,),
        in_specs=in_specs,
        out_specs=[o[1] for o in outs],
        out_shape=[o[0] for o in outs],
        compiler_params=_params(("parallel",)),
        name="project",
    )(*args)


def _cumsum_kernel(x_ref, cumt_o, aux_o, carry_ref):
    t = x_ref.shape[1]

    @pl.when(pl.program_id(1) == 0)
    def _():
        carry_ref[...] = jnp.zeros_like(carry_ref)

    c = _dot3(_tri(t, upper=False), x_ref[0], a_is_f32=False) + carry_ref[...]
    carry_ref[...] = c[t - 1:t, :]
    cumt_o[0] = c.T[0:N_HEADS]
    n_hi, n_mid, n_lo = (piece.astype(F32) for piece in _split3(-c * LOG2E))
    lane = lax.broadcasted_iota(I32, (t, LANES), 1)
    aux = jnp.where(lane < N_HEADS, n_hi,
                    jnp.where(lane < 2 * N_HEADS, n_mid,
                              jnp.where(lane < L_ONE, n_lo,
                                        jnp.where(lane < L_ONE + N_SPLIT, 1.0, 0.0))))
    aux_o[0] = aux.astype(BF16)


def _cumsum(x, t):
    b, l, w = x.shape
    return pl.pallas_call(
        _cumsum_kernel,
        grid=(b, l // t),
        in_specs=[pl.BlockSpec((1, t, w), lambda i, j: (i, j, 0))],
        out_specs=[pl.BlockSpec((1, N_HEADS, t), lambda i, j: (i, 0, j)),
                   pl.BlockSpec((1, t, w), lambda i, j: (i, j, 0))],
        out_shape=[jax.ShapeDtypeStruct((b, N_HEADS, l), F32), jax.ShapeDtypeStruct((b, l, w), BF16)],
        scratch_shapes=[pltpu.VMEM((1, w), F32)],
        compiler_params=_params(("parallel", "arbitrary")),
        name="cumsum",
    )(x)


def _sortable(x):
    b = lax.bitcast_convert_type(x + 0.0, I32)
    return b ^ ((b >> 31) & 0x7FFFFFFF)


def _sign_step(count_ge, n_total, shape, k, lowest):
    zero = jnp.zeros(shape, I32)
    c0 = count_ge(zero)
    ok0 = c0 >= k
    return jnp.where(ok0, zero, jnp.full(shape, lowest, I32)), jnp.where(ok0, c0, n_total)


def _descend(count_ge, t, n, n_bits, k):
    def cond(st):
        b, _, n = st
        return jnp.logical_and(b < n_bits, jnp.max(jnp.where(n != k, 1, 0)) > 0)

    def body(st):
        b, t, n = st
        for _ in range(BITS_PER_CHECK):
            bit = jnp.where(b < n_bits, jnp.left_shift(jnp.int32(1), jnp.maximum(n_bits - 1 - b, 0)), 0)
            cand = t | bit
            c = count_ge(cand)
            ok = c >= k
            b, t, n = b + 1, jnp.where(ok, cand, t), jnp.where(ok, c, n)
        return b, t, n

    _, t, n = lax.while_loop(cond, body, (jnp.int32(0), t, n))
    return t, n


def _kth_threshold(count_ge, n_total, shape, k, count_ge_hi=None):
    if count_ge_hi is None:
        t, n = _sign_step(count_ge, n_total, shape, k, INT_MIN)
        return _descend(count_ge, t, n, 31, k)
    t, n = _sign_step(count_ge_hi, n_total, shape, k, -2 ** 15)
    t, n = _descend(count_ge_hi, t, n, 15, k)
    return _descend(count_ge, jnp.left_shift(t, 16), n, 16, k)


def _overflowing(n_ge, thr, k):
    neg_key = _sortable(jnp.full((1, 1), NEG, F32))
    return jnp.where(n_ge > k, jnp.where(thr > neg_key, 1, 0), 0)


def _init_state_t(tq):
    return jnp.full((N_HEADS, tq), NEG, F32), jnp.zeros((D_VT, tq), F32)


def _online_update_t(state, logits, values):
    m_all, acc_all = state
    m_out, acc_out = [], []
    for h, s in enumerate(logits):
        m_prev = m_all[h:h + 1, :]
        m_new = jnp.maximum(m_prev, jnp.max(s, axis=0, keepdims=True))
        p = jnp.exp2(s - m_new).astype(BF16)
        m_out.append(m_new)
        acc_out.append(jnp.exp2(m_prev - m_new) * acc_all[VT_ROWS * h:VT_ROWS * (h + 1)]
                       + jnp.dot(values(h), p, preferred_element_type=F32))
    return jnp.concatenate(m_out, axis=0), jnp.concatenate(acc_out, axis=0)


def _flash_blocks(n_blocks, tq, logits_of, consume, s_ref):
    def put(slot, vals):
        for h, v in enumerate(vals):
            s_ref[slot, h] = v

    def get(slot):
        return [s_ref[slot, h] for h in range(N_HEADS)]

    def pair(jj, state):
        j = 2 * jj
        put(1, logits_of(j + 1))
        state = consume(j, state, get(0), False)
        put(0, logits_of(j + 2))
        return consume(j + 1, state, get(1), False)

    def one_left(state):
        return consume(n_blocks - 1, state, get(0), True)

    def two_left(state):
        put(1, logits_of(n_blocks - 1))
        state = consume(n_blocks - 2, state, get(0), False)
        return consume(n_blocks - 1, state, get(1), True)

    put(0, logits_of(0))
    state = lax.fori_loop(0, (n_blocks - 1) // 2, pair, _init_state_t(tq))
    return lax.cond(n_blocks % 2 == 1, one_left, two_left, state)


def _write_heads_t(o_ref, state):
    _, acc_all = state
    out_t = jnp.concatenate(
        [acc_all[VT_ROWS * h:VT_ROWS * h + HEAD_DIM] / acc_all[VT_ROWS * h + HEAD_DIM:VT_ROWS * h + HEAD_DIM + 1]
         for h in range(N_HEADS)], axis=0)
    o_ref[0] = out_t.T.astype(o_ref.dtype)


def _padded_head_queries_t(qt):
    zeros = jnp.zeros((HEAD_DIM, qt.shape[1]), qt.dtype)
    out = []
    for h in range(N_HEADS):
        qh = qt[HEAD_DIM * h:HEAD_DIM * (h + 1)]
        out.append(jnp.concatenate([qh, zeros] if h % 2 == 0 else [zeros, qh], axis=0))
    return out


def _fox_kernel(qt_ref, k_ref, aux_ref, vt_ref, cumt_ref, o_ref, s_ref, *, tk):
    tq = qt_ref.shape[1]
    i = pl.program_id(1)
    cum_q = cumt_ref[0]
    row = lax.broadcasted_iota(I32, (LANES, tq), 0)
    rhs = []
    for h, qpad in enumerate(_padded_head_queries_t(qt_ref[...])):
        c_hi, c_mid, c_lo = (piece.astype(F32) for piece in _split3(cum_q[h:h + 1, :] * LOG2E))
        pick = jnp.where(row < L_ONE, jnp.where(row % N_HEADS == h, 1.0, 0.0), 0.0)
        sel = jnp.where(row == L_ONE, c_hi,
                        jnp.where(row == L_ONE + 1, c_mid, jnp.where(row == L_ONE + 2, c_lo, pick)))
        rhs.append(jnp.concatenate([qpad, sel.astype(BF16)], axis=0))
    q_pos = i * tq + lax.broadcasted_iota(I32, (1, tq), 1)

    def logits_of(j):
        c0 = pl.multiple_of(j * tk, tk)
        aux = aux_ref[0, pl.ds(c0, tk), :]
        out = []
        for h in range(N_HEADS):
            pair = h // 2
            lhs = jnp.concatenate([k_ref[0, pl.ds(c0, tk), LANES * pair:LANES * (pair + 1)], aux], axis=1)
            out.append(jnp.dot(lhs, rhs[h], preferred_element_type=F32))
        return out

    def consume(j, state, logits, last):
        c0 = pl.multiple_of(j * tk, tk)
        if last:
            keep = (c0 + lax.broadcasted_iota(I32, (tk, 1), 0)) <= q_pos
            logits = [jnp.where(keep, s, NEG) for s in logits]
        return _online_update_t(state, logits, lambda h: vt_ref[VT_ROWS * h:VT_ROWS * (h + 1), pl.ds(c0, tk)])

    n_full = (i * tq) // tk
    _write_heads_t(o_ref, _flash_blocks(n_full + 1, tq, logits_of, consume, s_ref))


def _fox_prompt(qt, k, aux, vt, cumt, tq, tk):
    b, l, d = k.shape
    nq = l // tq
    return pl.pallas_call(
        functools.partial(_fox_kernel, tk=tk),
        grid=(b, nq),
        in_specs=[pl.BlockSpec((d, tq), lambda bi, i: (0, bi * nq + i)),
                  _per_batch_spec((1, l, d), lambda bi, i: (bi, 0, 0)),
                  _per_batch_spec((1, l, LANES), lambda bi, i: (bi, 0, 0)),
                  _per_batch_spec((D_VT, l), lambda bi, i: (0, bi)),
                  pl.BlockSpec((1, N_HEADS, tq), lambda bi, i: (bi, 0, i))],
        out_specs=pl.BlockSpec((1, tq, d), lambda bi, i: (bi, i, 0)),
        out_shape=jax.ShapeDtypeStruct((b, l, d), BF16),
        scratch_shapes=[pltpu.VMEM((2, N_HEADS, tk, tq), F32)],
        compiler_params=_params(("parallel", "arbitrary")),
        name="fox_prompt",
    )(qt, k, aux, vt, cumt)


def _dsa_kernel(iqt_ref, ikr_ref, iwt_ref, qt_ref, k_ref, vt_ref, o_ref, key_ref, key_hi_ref, s_ref, *,
                tk, topk):
    tq = qt_ref.shape[1]
    i = pl.program_id(1)
    n_blk = (i * tq) // tk + 1
    q_chunk = (i * tq + lax.broadcasted_iota(I32, (1, tq), 1)) // CHUNK

    def admissible(c0):
        return (c0 + lax.broadcasted_iota(I32, (tk, 1), 0)) // CHUNK <= q_chunk

    iqt = iqt_ref[...]
    iq_rhs = []
    for h in range(N_IDX_HEADS):
        parts = [jnp.zeros((IDX_DIM * h, tq), iqt.dtype), iqt[IDX_DIM * h:IDX_DIM * (h + 1)],
                 jnp.zeros((IDX_DIM * (N_IDX_HEADS - 1 - h), tq), iqt.dtype)]
        iq_rhs.append(jnp.concatenate([p for p in parts if p.shape[0]], axis=0))
    iw = iwt_ref[0]

    def score_block(j, carry):
        c0 = pl.multiple_of(j * tk, tk)
        kb = ikr_ref[0, pl.ds(c0, tk), :]
        logits = [jnp.dot(kb, iq_rhs[h], preferred_element_type=F32) for h in range(N_IDX_HEADS)]
        score = iw[0:1, :] * jnp.maximum(logits[0], 0.0)
        for h in range(1, N_IDX_HEADS):
            score = score + iw[h:h + 1, :] * jnp.maximum(logits[h], 0.0)
        keys = _sortable(jnp.where(admissible(c0), score, NEG))
        key_ref[pl.ds(c0, tk), :] = keys
        key_hi_ref[pl.ds(c0, tk), :] = (keys >> 16).astype(I16)
        return carry

    lax.fori_loop(0, n_blk, score_block, 0)

    def count(pred):
        def body(j, acc):
            c0 = pl.multiple_of(j * tk, tk)
            hit = jnp.where(pred(key_ref[pl.ds(c0, tk), :]), 1, 0).astype(I32)
            return acc + jnp.sum(hit.reshape(tk // SUBLANES, SUBLANES, tq), axis=0)
        part = lax.fori_loop(0, n_blk, body, jnp.zeros((SUBLANES, tq), I32))
        return jnp.sum(part, axis=0, keepdims=True)

    def count_hi(t):
        rows = 2 * SUBLANES
        t16 = jnp.broadcast_to(t, (rows, tq)).astype(I16)
        one = jnp.ones((tk, tq), I16)
        zero = jnp.zeros((tk, tq), I16)

        def body(j, acc):
            c0 = pl.multiple_of(j * tk, tk)
            kb = key_hi_ref[pl.ds(c0, tk), :].reshape(tk // rows, rows, tq)
            hit = jnp.where(kb >= t16, one.reshape(kb.shape), zero.reshape(kb.shape))
            for g in range(tk // rows):
                acc = acc + hit[g]
            return acc
        part = lax.fori_loop(0, n_blk, body, jnp.zeros((rows, tq), I16))
        return jnp.sum(part.astype(I32), axis=0, keepdims=True)

    thr, n_ge = _kth_threshold(lambda t: count(lambda kb: kb >= t), n_blk * tk, (1, tq), topk,
                               count_ge_hi=count_hi)

    @pl.when(jnp.max(_overflowing(n_ge, thr, topk)) > 0)
    def _():
        take = (topk - count(lambda kb: kb > thr)).astype(F32)
        strict_lower = (lax.broadcasted_iota(I32, (tk, tk), 0)
                        > lax.broadcasted_iota(I32, (tk, tk), 1)).astype(BF16)

        def body(j, seen):
            c0 = pl.multiple_of(j * tk, tk)
            kb = key_ref[pl.ds(c0, tk), :]
            tie = kb == thr
            tie_f = jnp.where(tie, 1.0, 0.0)
            rank = jnp.dot(strict_lower, tie_f.astype(BF16), preferred_element_type=F32) + seen
            drop = jnp.where(tie, jnp.where(rank >= take, 1, 0), 0)
            key_ref[pl.ds(c0, tk), :] = jnp.where(drop > 0, INT_MIN, kb)
            return seen + jnp.sum(tie_f, axis=0, keepdims=True)

        lax.fori_loop(0, n_blk, body, jnp.zeros((1, tq), F32))

    rhs = _padded_head_queries_t(qt_ref[...])

    def logits_of(j):
        c0 = pl.multiple_of(j * tk, tk)
        return [jnp.dot(k_ref[0, pl.ds(c0, tk), LANES * (h // 2):LANES * (h // 2 + 1)], rhs[h],
                        preferred_element_type=F32) for h in range(N_HEADS)]

    def consume(j, state, logits, last):
        c0 = pl.multiple_of(j * tk, tk)
        bias = jnp.where(key_ref[pl.ds(c0, tk), :] >= thr, jnp.where(admissible(c0), 0.0, NEG), NEG)
        return _online_update_t(state, [s + bias for s in logits],
                                lambda h: vt_ref[VT_ROWS * h:VT_ROWS * (h + 1), pl.ds(c0, tk)])

    _write_heads_t(o_ref, _flash_blocks(n_blk, tq, logits_of, consume, s_ref))


def _dsa_prompt(iqt, ikr, iwt, qt, k, vt, tq, tk, topk):
    b, l, d = k.shape
    nq = l // tq
    w_idx = ikr.shape[2]
    qcol = lambda n: pl.BlockSpec((n, tq), lambda bi, i: (0, bi * nq + i))
    full = lambda n: _per_batch_spec((1, l, n), lambda bi, i: (bi, 0, 0))
    return pl.pallas_call(
        functools.partial(_dsa_kernel, tk=tk, topk=topk),
        grid=(b, nq),
        in_specs=[qcol(w_idx), full(w_idx), pl.BlockSpec((1, N_IDX_HEADS, tq), lambda bi, i: (bi, 0, i)),
                  qcol(d), full(d), _per_batch_spec((D_VT, l), lambda bi, i: (0, bi))],
        out_specs=pl.BlockSpec((1, tq, d), lambda bi, i: (bi, i, 0)),
        out_shape=jax.ShapeDtypeStruct((b, l, d), BF16),
        scratch_shapes=[pltpu.VMEM((l, tq), I32), pltpu.VMEM((l, tq), I16),
                        pltpu.VMEM((2, N_HEADS, tk, tq), F32)],
        compiler_params=_params(("parallel", "arbitrary")),
        name="dsa_prompt",
    )(iqt, ikr, iwt, qt, k, vt)


def _pair_queries(q, pair):
    t = q.shape[0]
    low = lax.broadcasted_iota(I32, (t, LANES), 1) < HEAD_DIM
    zero = jnp.zeros((t, LANES), q.dtype)
    slab = q[:, LANES * pair:LANES * (pair + 1)]
    return jnp.concatenate([jnp.where(low, slab, zero), jnp.where(low, zero, slab)], axis=0)


def _nt_dot(a, b):
    return lax.dot_general(a, b, (((1,), (1,)), ((), ())), preferred_element_type=F32)


def _lane_fold_count(mask):
    ones = jnp.where(mask, 1, 0).astype(I32)
    acc = ones[:, 0:LANES]
    for c in range(1, mask.shape[1] // LANES):
        acc = acc + ones[:, LANES * c:LANES * (c + 1)]
    return acc


def _pair_attention(q_pair, kt_past, vt_past, k_new, v_new, bias_past, bias_new):
    t = k_new.shape[0]
    s_past = jnp.dot(q_pair, kt_past, preferred_element_type=F32) + bias_past
    s_new = _nt_dot(q_pair, k_new) + bias_new
    m = jnp.maximum(jnp.max(s_past, axis=1, keepdims=True), jnp.max(s_new, axis=1, keepdims=True))
    p_past = jnp.exp(s_past - m)
    p_new = jnp.exp(s_new - m)
    denom = jnp.sum(p_past, axis=1, keepdims=True) + jnp.sum(p_new, axis=1, keepdims=True)
    o = (_nt_dot(p_past.astype(BF16), vt_past)
         + jnp.dot(p_new.astype(BF16), v_new, preferred_element_type=F32)) / denom
    low = lax.broadcasted_iota(I32, (t, LANES), 1) < HEAD_DIM
    return jnp.where(low, o[0:t], o[t:2 * t])


def _sample_kernel(fq_ref, fkn_ref, fvn_ref, dq_ref, dkn_ref, dvn_ref, iq_ref, ikrn_ref, misc_ref,
                   cfk_ref, cfv_ref, cdk_ref, cdv_ref, cik_ref, clogf_ref,
                   o_ref, kp_ref, kn_ref, *, topk):
    t = fq_ref.shape[1]
    p_len = cfk_ref.shape[2]
    n_pb = p_len // LANES
    row = lax.broadcasted_iota(I32, (t, 1), 0)
    col_new = lax.broadcasted_iota(I32, (1, t), 1)
    stack2 = lambda a: jnp.concatenate([a, a], axis=0)

    n_rows = N_HEADS * n_pb
    within = _dot3(clogf_ref[0], _tri(LANES, upper=True), a_is_f32=True)
    block_tot = jnp.broadcast_to(within[:, LANES - 1:LANES], (n_rows, LANES))
    r = lax.broadcasted_iota(I32, (n_rows, n_rows), 0)
    c = lax.broadcasted_iota(I32, (n_rows, n_rows), 1)
    same_head = (r // n_pb) == (c // n_pb)
    earlier = jnp.where(same_head, jnp.where(c < r, 1.0, 0.0), 0.0).astype(BF16)
    cum_past = within + _dot3(earlier, block_tot, a_is_f32=False)
    head_tot = _dot3(jnp.where(same_head, 1.0, 0.0).astype(BF16), block_tot, a_is_f32=False)
    misc = misc_ref[0]
    pre_new = _dot3(_tri(t, upper=False), misc, a_is_f32=False)
    pre_new_t = _dot3(misc.T, _tri(t, upper=True), a_is_f32=True)
    cq, ck_past, ck_new = [], [], []
    for h in range(N_HEADS):
        tot = head_tot[h * n_pb:h * n_pb + 1, :]
        cq.append(pre_new[:, h:h + 1] + tot[:, 0:1])
        ck_past.append(jnp.concatenate([cum_past[h * n_pb + b:h * n_pb + b + 1, :] for b in range(n_pb)], axis=1))
        ck_new.append(pre_new_t[h:h + 1, :] + tot[:, 0:t])

    causal_new = col_new <= row
    fq = fq_ref[0]
    for pair in range(N_PAIRS):
        sl = slice(LANES * pair, LANES * (pair + 1))
        h0, h1 = 2 * pair, 2 * pair + 1
        bias_past = jnp.concatenate([cq[h0] - ck_past[h0], cq[h1] - ck_past[h1]], axis=0)
        bias_new = jnp.concatenate([jnp.where(causal_new, cq[h0] - ck_new[h0], NEG),
                                    jnp.where(causal_new, cq[h1] - ck_new[h1], NEG)], axis=0)
        o_ref[0, :, sl] = _pair_attention(
            _pair_queries(fq, pair), cfk_ref[0, sl, :].astype(BF16), cfv_ref[0, sl, :].astype(BF16),
            fkn_ref[0, :, sl], fvn_ref[0, :, sl], bias_past, bias_new).astype(o_ref.dtype)

    iq = iq_ref[0]
    lane_head = lax.broadcasted_iota(I32, iq.shape, 1) // IDX_DIM
    iq_stack = jnp.concatenate([jnp.where(lane_head == h, iq, jnp.zeros_like(iq))
                                for h in range(N_IDX_HEADS)], axis=0)
    ikt = cik_ref[0].astype(BF16)
    lg_past = jnp.dot(iq_stack, jnp.concatenate([ikt] * N_IDX_HEADS, axis=0), preferred_element_type=F32)
    lg_new = _nt_dot(iq_stack, ikrn_ref[0])
    sc_past = jnp.zeros((t, p_len), F32)
    sc_new = jnp.zeros((t, t), F32)
    for h in range(N_IDX_HEADS):
        w = misc[:, L_IW + h:L_IW + h + 1]
        sc_past = sc_past + w * jnp.maximum(lg_past[h * t:(h + 1) * t], 0.0)
        sc_new = sc_new + w * jnp.maximum(lg_new[h * t:(h + 1) * t], 0.0)
    row_chunk = (p_len + row) // CHUNK
    adm_past = (lax.broadcasted_iota(I32, (1, p_len), 1) // CHUNK) <= row_chunk
    adm_new = ((p_len + col_new) // CHUNK) <= row_chunk
    kp_ref[...] = _sortable(jnp.where(adm_past, sc_past, NEG))
    kn_ref[...] = _sortable(jnp.where(adm_new, sc_new, NEG))

    def count(pred):
        c_past = jnp.sum(_lane_fold_count(pred(kp_ref[...])), axis=1, keepdims=True)
        return c_past + jnp.sum(jnp.where(pred(kn_ref[...]), 1, 0).astype(I32), axis=1, keepdims=True)

    thr, n_ge = _kth_threshold(lambda x: count(lambda kb: kb >= x), p_len + t, (t, 1), topk)

    @pl.when(jnp.max(_overflowing(n_ge, thr, topk)) > 0)
    def _():
        take = (topk - count(lambda kb: kb > thr)).astype(F32)
        seen = jnp.zeros((t, 1), F32)
        for ref, n in ((kp_ref, p_len), (kn_ref, t)):
            strict_upper = (lax.broadcasted_iota(I32, (n, n), 0)
                            < lax.broadcasted_iota(I32, (n, n), 1)).astype(BF16)
            kb = ref[...]
            tie = kb == thr
            tie_f = jnp.where(tie, 1.0, 0.0)
            rank = jnp.dot(tie_f.astype(BF16), strict_upper, preferred_element_type=F32) + seen
            drop = jnp.where(tie, jnp.where(rank >= take, 1, 0), 0)
            ref[...] = jnp.where(drop > 0, INT_MIN, kb)
            seen = seen + jnp.sum(tie_f, axis=1, keepdims=True)

    bias_past = stack2(jnp.where(kp_ref[...] >= thr, jnp.where(adm_past, 0.0, NEG), NEG))
    bias_new = stack2(jnp.where(kn_ref[...] >= thr, jnp.where(adm_new, 0.0, NEG), NEG))
    dq = dq_ref[0]
    for pair in range(N_PAIRS):
        sl = slice(LANES * pair, LANES * (pair + 1))
        o_ref[0, :, D_HEADS + LANES * pair:D_HEADS + LANES * (pair + 1)] = _pair_attention(
            _pair_queries(dq, pair), cdk_ref[0, sl, :].astype(BF16), cdv_ref[0, sl, :].astype(BF16),
            dkn_ref[0, :, sl], dvn_ref[0, :, sl], bias_past, bias_new).astype(o_ref.dtype)


def _sample_mixers(fq, fkn, fvn, dq, dkn, dvn, iq, ikrn, misc, cfk, cfv, cdk, cdv, cik, clogf, topk):
    b, t, d = fq.shape
    p_len = cfk.shape[2]
    new = lambda n: pl.BlockSpec((1, t, n), lambda bi: (bi, 0, 0))
    whole = lambda a: pl.BlockSpec((1,) + a.shape[1:], lambda bi: (bi, 0, 0))
    return pl.pallas_call(
        functools.partial(_sample_kernel, topk=topk),
        grid=(b,),
        in_specs=[new(d), new(d), new(d), new(d), new(d), new(d), new(iq.shape[2]), new(ikrn.shape[2]), new(LANES),
                  whole(cfk), whole(cfv), whole(cdk), whole(cdv), whole(cik), whole(clogf)],
        out_specs=pl.BlockSpec((1, t, 2 * d), lambda bi: (bi, 0, 0)),
        out_shape=jax.ShapeDtypeStruct((b, t, 2 * d), BF16),
        scratch_shapes=[pltpu.VMEM((t, p_len), I32), pltpu.VMEM((t, t), I32)],
        compiler_params=_params(("parallel",)),
        name="sample_mixers",
    )(fq, fkn, fvn, dq, dkn, dvn, iq, ikrn, misc, cfk, cfv, cdk, cdv, cik, clogf)


def _layer_norm(x, g, b):
    mu = jnp.mean(x, axis=-1, keepdims=True)
    xc = x - mu
    var = jnp.mean(xc * xc, axis=-1, keepdims=True)
    return xc * lax.rsqrt(var + LN_EPS) * g + b


def _finish_kernel(x_ref, mixf_ref, mixd_ref, p_ref, wo_ref, g1_ref, b1_ref, wup_ref, wdn_ref,
                   g2_ref, b2_ref, wple_ref, wg_ref, bg_ref, o_ref, *, alpha, ff_chunk):
    d_mix_half = mixf_ref.shape[1]
    a = (alpha * x_ref[...]
         + jnp.dot(mixf_ref[...], wo_ref[0:d_mix_half, :], preferred_element_type=F32)
         + jnp.dot(mixd_ref[...], wo_ref[d_mix_half:, :], preferred_element_type=F32))
    x1 = _layer_norm(a, g1_ref[...], b1_ref[...])
    x1b = x1.astype(BF16)
    ffn = jnp.zeros_like(x1)
    for c in range(wup_ref.shape[1] // ff_chunk):
        sl = slice(ff_chunk * c, ff_chunk * (c + 1))
        hid = jnp.maximum(jnp.dot(x1b, wup_ref[:, sl], preferred_element_type=F32), 0.0)
        ffn = ffn + jnp.dot((hid * hid).astype(BF16), wdn_ref[sl, :], preferred_element_type=F32)
    x2 = _layer_norm(alpha * x1 + ffn, g2_ref[...], b2_ref[...])
    gate = jax.nn.sigmoid(jnp.dot(x2.astype(BF16), wg_ref[...], preferred_element_type=F32) + bg_ref[...])
    pe = jnp.dot(p_ref[...].astype(BF16), wple_ref[...], preferred_element_type=F32)
    o_ref[...] = x2 + gate * pe


def _finish(x2d, mixes, p2d, w, alpha, tm):
    m, d = x2d.shape
    tm = min(tm, m)
    row = lambda n: pl.BlockSpec((tm, n), lambda i: (i, 0))
    if len(mixes) == 1:
        half = mixes[0].shape[1] // 2
        mix_specs = [pl.BlockSpec((tm, half), lambda i: (i, 0)), pl.BlockSpec((tm, half), lambda i: (i, 1))]
        mix_args = [mixes[0], mixes[0]]
    else:
        mix_specs = [row(mixes[0].shape[1]), row(mixes[1].shape[1])]
        mix_args = list(mixes)
    d_ff = w["w_up"].shape[1]
    vec = lambda n: _const_spec((1, n))
    return pl.pallas_call(
        functools.partial(_finish_kernel, alpha=alpha, ff_chunk=min(d_ff, 1024)),
        grid=(m // tm,),
        in_specs=[row(d)] + mix_specs + [row(p2d.shape[1]),
                  _const_spec(w["w_o"].shape), vec(d), vec(d),
                  _const_spec(w["w_up"].shape), _const_spec(w["w_down"].shape), vec(d), vec(d),
                  _const_spec(w["w_ple"].shape), _const_spec(w["w_ple_gate"].shape), vec(d)],
        out_specs=row(d),
        out_shape=jax.ShapeDtypeStruct((m, d), F32),
        compiler_params=_params(("parallel",)),
        name="finish",
    )(x2d, *mix_args, p2d, w["w_o"], w["ln1_g"], w["ln1_b"], w["w_up"], w["w_down"],
      w["ln2_g"], w["ln2_b"], w["w_ple"], w["w_ple_gate"], w["b_ple_gate"])


def _relayout_w_in(w_in):
    widths = (("fq", D_HEADS), ("fk", D_HEADS), ("fv", D_HEADS), ("fg", N_HEADS),
              ("dq", D_HEADS), ("dk", D_HEADS), ("dv", D_HEADS),
              ("iq", W_IDX), ("ik", IDX_DIM), ("iw", N_IDX_HEADS))
    cols, o = {}, 0
    for name, n in widths:
        cols[name] = w_in[:, o:o + n]
        o += n
    pad = jnp.zeros((w_in.shape[0], LANES - N_SPLIT * N_HEADS - N_IDX_HEADS), w_in.dtype)
    w_r = jnp.concatenate([cols["fq"], cols["fk"], cols["fv"], cols["dq"], cols["dk"], cols["dv"],
                           cols["iq"], jnp.tile(cols["ik"], (1, N_IDX_HEADS)),
                           jnp.tile(cols["fg"], (1, N_SPLIT)), cols["iw"], pad], axis=1)
    return w_r.astype(BF16)


def _layer(x_p, x_s, p_p, p_s, c_fk, c_fv, c_logf, c_dk, c_dv, c_ik, w, alpha):
    b, l, d = x_p.shape
    bs, t, _ = x_s.shape
    p_len = c_fk.shape[1]
    w_r = _relayout_w_in(w["w_in"])

    (fk, dk, ikr, logf3, fqt, dqt, iqt, fvt, dvt,
     fkt32, fvt32, dkt32, dvt32, ikt32, logft, iwt) = _project(
        x_p.reshape(b * l, d), w_r, w["b_f"], jnp.arange(l, dtype=I32), tm=512, n_seq=b)
    r3 = lambda a: a.reshape((b, l) + a.shape[1:])
    cumt, aux = _cumsum(r3(logf3), 512)
    mix_fox = _fox_prompt(fqt, r3(fk), aux, fvt, cumt, tq=256, tk=512)
    mix_dsa = _dsa_prompt(iqt, r3(ikr), iwt, dqt, r3(dk), dvt, tq=256, tk=512, topk=min(TOPK_MAX, l // 4))
    y_p = _finish(x_p.reshape(b * l, d), (mix_fox.reshape(b * l, -1), mix_dsa.reshape(b * l, -1)),
                  p_p.reshape(b * l, -1), w, alpha, tm=512).reshape(b, l, d)
    heads_first = lambda a: jnp.transpose(a.reshape(b, N_HEADS, HEAD_DIM, l), (0, 3, 1, 2))
    pos_last = lambda a: jnp.swapaxes(a, 1, 2)
    rows_p = (heads_first(fkt32), heads_first(fvt32), pos_last(logft),
              heads_first(dkt32), heads_first(dvt32), pos_last(ikt32))

    (sfq, sfk, sfv, sdq, sdk, sdv, siq, sikr, smisc, sfk32, sfv32, sdk32, sdv32, sik32, slogf) = _project(
        x_s.reshape(bs * t, d), w_r, w["b_f"], p_len + jnp.arange(t, dtype=I32), tm=512, n_seq=0)
    s3 = lambda a: a.reshape((bs, t) + a.shape[1:])
    cache_t = lambda c: jnp.transpose(c, (0, 2, 3, 1)).reshape(bs, D_HEADS, p_len)
    clogf = jnp.swapaxes(c_logf.astype(F32), 1, 2).reshape(bs, N_HEADS * (p_len // LANES), LANES)
    mix_s = _sample_mixers(s3(sfq), s3(sfk), s3(sfv), s3(sdq), s3(sdk), s3(sdv), s3(siq), s3(sikr), s3(smisc),
                           cache_t(c_fk), cache_t(c_fv), cache_t(c_dk), cache_t(c_dv),
                           jnp.swapaxes(c_ik, 1, 2), clogf, topk=min(TOPK_MAX, (p_len + t) // 4))
    y_s = _finish(x_s.reshape(bs * t, d), (mix_s.reshape(bs * t, -1),), p_s.reshape(bs * t, -1),
                  w, alpha, tm=512).reshape(bs, t, d)
    heads_s = lambda a: a.reshape(bs, t, N_HEADS, HEAD_DIM)
    rows_s = (heads_s(sfk32), heads_s(sfv32), s3(slogf), heads_s(sdk32), heads_s(sdv32), s3(sik32))
    return y_p, y_s, rows_p, rows_s


def kernel(x_prompt, x_sample, p_prompt, p_sample, cache_fox_k, cache_fox_v, cache_fox_logf, cache_dsa_k, cache_dsa_v, cache_idx_k, w_in, b_f, w_o, ln1_g, ln1_b, w_up, w_down, ln2_g, ln2_b, w_ple, w_ple_gate, b_ple_gate):
    depth = w_in.shape[0]
    alpha = (2 * depth) ** 0.25
    y_p, y_s = x_prompt, x_sample
    new_rows = [[] for _ in range(12)]
    for i in range(depth):
        vec = lambda a: a[i].reshape(1, -1)
        w = {"w_in": w_in[i], "b_f": b_f[i], "w_o": w_o[i].astype(BF16),
             "ln1_g": vec(ln1_g), "ln1_b": vec(ln1_b),
             "w_up": w_up[i].astype(BF16), "w_down": w_down[i].astype(BF16),
             "ln2_g": vec(ln2_g), "ln2_b": vec(ln2_b),
             "w_ple": w_ple[i].astype(BF16), "w_ple_gate": w_ple_gate[i].astype(BF16),
             "b_ple_gate": vec(b_ple_gate)}
        y_p, y_s, rows_p, rows_s = _layer(
            y_p, y_s, p_prompt[i], p_sample[i], cache_fox_k[i], cache_fox_v[i], cache_fox_logf[i],
            cache_dsa_k[i], cache_dsa_v[i], cache_idx_k[i], w, alpha)
        for lst, a in zip(new_rows, rows_p + rows_s):
            lst.append(a)
    return (y_p, y_s) + tuple(jnp.stack(a, axis=0) for a in new_rows)
```
